```python
import jax, jax.numpy as jnp
from jax import lax
import numpy as np

D_MODEL = 1024
BATCH = 32
SEQ = 2048
DEPTH = 1

N_META = 16
D_MIX = D_MODEL
D_CONV = D_MIX // 2
D_POOL = D_MIX - D_CONV
CONV_HEADS = 8
CONV_WIDTH = 3
POOL_WINDOWS = (2, 4, 8, 16)
N_POOL_GROUPS = len(POOL_WINDOWS)
POOL_GROUP = D_POOL // N_POOL_GROUPS
D_IN_PROJ = 3 * D_CONV + D_POOL
D_FF = ((int(np.ceil(8 * D_MODEL / 3)) + 255) // 256) * 256
RMS_EPS = 1e-6

kernel_name = "hymba_conv_pool_hybrid_block"


def rms_norm(x, g):
    xf = x.astype(jnp.float32)
    y = xf * lax.rsqrt(jnp.mean(xf * xf, axis=-1, keepdims=True) + RMS_EPS)
    return (y * g.astype(jnp.float32)).astype(x.dtype)


def causal_short_conv(u, w):
    k_width = w.shape[0]
    seq_len = u.shape[1]
    up = jnp.pad(u, ((0, 0), (k_width - 1, 0), (0, 0)))
    y = w[0] * up[:, 0:seq_len]
    for k in range(1, k_width):
        y = y + w[k] * up[:, k:k + seq_len]
    return y


def multiscale_pool(u, pool_w, pool_scale):
    bsz, seq_len, _ = u.shape
    ug = u.reshape(bsz, seq_len, N_POOL_GROUPS, POOL_GROUP)
    pos = jnp.arange(seq_len)
    outs = []
    for g, win in enumerate(POOL_WINDOWS):
        xg = ug[:, :, g].astype(jnp.float32)
        cs = jnp.cumsum(xg, axis=1)
        cs_prev = jnp.pad(cs, ((0, 0), (win, 0), (0, 0)))[:, :seq_len]
        cnt = jnp.minimum(pos + 1, win).astype(jnp.float32)[None, :, None]
        outs.append((cs - cs_prev) / cnt - xg)
    pooled = jnp.stack(outs, axis=2).astype(u.dtype)
    mixed = jnp.einsum('blgc,gcd->blgd', pooled, pool_w)
    return mixed.reshape(bsz, seq_len, D_POOL) * pool_scale


def setup_inputs(seed: int = 0) -> dict:
    key = jax.random.key(seed)
    ks = jax.random.split(key, 16)
    f32 = jnp.float32

    def nrm(k, shape, scale):
        return jax.random.normal(k, shape, f32) * scale

    def gain(k):
        return 1.0 + 0.05 * jax.random.normal(k, (DEPTH, D_MODEL), f32)

    return {
        "x": jax.random.normal(ks[0], (BATCH, SEQ, D_MODEL), f32),
        "meta_tokens": nrm(ks[1], (N_META, D_MODEL), 1.0),
        "norm_mix_pre": gain(ks[2]),
        "w_in": nrm(ks[3], (DEPTH, D_MODEL, D_IN_PROJ), D_MODEL ** -0.5),
        "conv_w": nrm(ks[4], (DEPTH, CONV_WIDTH, D_CONV), CONV_WIDTH ** -0.5),
        "pool_w": nrm(ks[5], (DEPTH, N_POOL_GROUPS, POOL_GROUP, POOL_GROUP), POOL_GROUP ** -0.5),
        "pool_scale": 1.0 + 0.1 * jax.random.normal(ks[6], (DEPTH, D_POOL), f32),
        "w_out": nrm(ks[7], (DEPTH, D_MIX, D_MODEL), D_MIX ** -0.5),
        "norm_mix_post": gain(ks[8]),
        "norm_ffn_pre": gain(ks[9]),
        "w_gate": nrm(ks[10], (DEPTH, D_MODEL, D_FF), D_MODEL ** -0.5),
        "w_up": nrm(ks[11], (DEPTH, D_MODEL, D_FF), D_MODEL ** -0.5),
        "w_down": nrm(ks[12], (DEPTH, D_FF, D_MODEL), D_FF ** -0.5),
        "norm_ffn_post": gain(ks[13]),
    }


def reference(x, meta_tokens, norm_mix_pre, w_in, conv_w, pool_w, pool_scale, w_out,
              norm_mix_post, norm_ffn_pre, w_gate, w_up, w_down, norm_ffn_post):
    bsz = x.shape[0]
    meta = jnp.broadcast_to(meta_tokens[None].astype(x.dtype), (bsz, N_META, D_MODEL))
    h = jnp.concatenate([meta, x], axis=1)

    for i in range(DEPTH):
        a = rms_norm(h, norm_mix_pre[i])
        z = a @ w_in[i]
        b_gate = z[..., 0:D_CONV]
        c_gate = z[..., D_CONV:2 * D_CONV]
        v = z[..., 2 * D_CONV:3 * D_CONV]
        p = z[..., 3 * D_CONV:]
        y_conv = b_gate * causal_short_conv(c_gate * v, conv_w[i])
        y_pool = multiscale_pool(p, pool_w[i], pool_scale[i])
        m = jnp.concatenate([y_conv, y_pool], axis=-1) @ w_out[i]
        h = h + rms_norm(m, norm_mix_post[i])

        f = rms_norm(h, norm_ffn_pre[i])
        g = jax.nn.silu(f @ w_gate[i]) * (f @ w_up[i])
        h = h + rms_norm(g @ w_down[i], norm_ffn_post[i])

    return h[:, N_META:]
```

```python
import functools

import jax
import jax.numpy as jnp
from jax import lax
from jax.experimental import pallas as pl
from jax.experimental.pallas import tpu as pltpu

RMS_EPS = 1e-6
POOL_WINDOWS = (2, 4, 8, 16)
N_META = 16
SUBLANES = 8
LANES = 128
MXU_COLS = 256
CONV_HIST = SUBLANES
POOL_HIST = 16
SEQ_TILE = 512
VMEM_LIMIT_BYTES = 56 * 1024 * 1024

_BF16 = jnp.bfloat16
_F32 = jnp.float32


def _rms_norm(x, g):
    y = x * lax.rsqrt(jnp.mean(x * x, axis=-1, keepdims=True) + RMS_EPS)
    return y * g


def _dot(a, b):
    return jnp.dot(a, b, preferred_element_type=_F32)


def _project_in(x, g_ref, win_ref):
    d_conv = win_ref.shape[1] // 4
    z = _dot(_rms_norm(x, g_ref[...]).astype(_BF16), win_ref[...])
    b_gate = z[:, 0:d_conv]
    cv = z[:, d_conv:2 * d_conv] * z[:, 2 * d_conv:3 * d_conv]
    p = z[:, 3 * d_conv:]
    return b_gate, cv, p


def _meta_kernel(meta_ref, g_ref, win_ref, cv_ref, p_ref):
    _, cv, p = _project_in(meta_ref[...], g_ref, win_ref)
    cv_ref[...] = cv[N_META - CONV_HIST:]
    p_ref[...] = p[N_META - POOL_HIST:]


def _causal_conv(cv_hist, cv, w):
    u = jnp.concatenate([cv_hist, cv], axis=0)
    k = w.shape[0]
    y = w[0:1] * pltpu.roll(u, k - 1, 0)
    for i in range(1, k):
        shifted = pltpu.roll(u, k - 1 - i, 0) if i < k - 1 else u
        y = y + w[i:i + 1] * shifted
    return y[CONV_HIST:]


def _trailing_mean_minus_token(p_hist, p):
    group = p.shape[1] // len(POOL_WINDOWS)
    s = jnp.concatenate([p_hist, p], axis=0)
    means = []
    win = 1
    for target in POOL_WINDOWS:
        while win < target:
            s = s + pltpu.roll(s, win, 0)
            win *= 2
        means.append(s[:, 0:group] * (1.0 / target))
        s = s[:, group:]
    return jnp.concatenate(means, axis=1)[POOL_HIST:] - p


def _block_kernel(x_ref, mcv_ref, mp_ref, g_mix_pre_ref, win_ref, convw_ref,
                  poolw_ref, pscale_ref, wout_ref, g_mix_post_ref,
                  g_ffn_pre_ref, wg_ref, wu_ref, wd_ref, g_ffn_post_ref,
                  o_ref, cv_hist_ref, p_hist_ref):
    tile = x_ref.shape[1]

    @pl.when(pl.program_id(1) == 0)
    def _():
        cv_hist_ref[...] = mcv_ref[...]
        p_hist_ref[...] = mp_ref[...]

    x = x_ref[0]

    b_gate, cv, p = _project_in(x, g_mix_pre_ref, win_ref)
    y_conv = b_gate * _causal_conv(cv_hist_ref[...], cv, convw_ref[...])
    pooled = _trailing_mean_minus_token(p_hist_ref[...], p)
    cv_hist_ref[...] = cv[tile - CONV_HIST:]
    p_hist_ref[...] = p[tile - POOL_HIST:]
    y_pool = _dot(pooled.astype(_BF16), poolw_ref[...]) * pscale_ref[...]
    mix_in = jnp.concatenate([y_conv, y_pool], axis=1).astype(_BF16)
    h = x + _rms_norm(_dot(mix_in, wout_ref[...]), g_mix_post_ref[...])

    f = _rms_norm(h, g_ffn_pre_ref[...]).astype(_BF16)
    d_ff = wg_ref.shape[1]
    acc = None
    for j in range(d_ff // MXU_COLS):
        cols = slice(j * MXU_COLS, (j + 1) * MXU_COLS)
        gate = _dot(f, wg_ref[:, cols])
        up = _dot(f, wu_ref[:, cols])
        act = (gate / (1.0 + jnp.exp(-gate)) * up).astype(_BF16)
        part = _dot(act, wd_ref[cols, :])
        acc = part if acc is None else acc + part
    o_ref[0] = h + _rms_norm(acc, g_ffn_post_ref[...])


def _resident(shape):
    return pl.BlockSpec(shape, lambda *_: (0,) * len(shape),
                        pipeline_mode=pl.Buffered(1))


def _block_diag(w):
    groups, n, _ = w.shape
    eye = jnp.eye(groups, dtype=w.dtype)
    return (eye[:, None, :, None] * w[:, :, None, :]).reshape(groups * n, groups * n)


@functools.partial(jax.jit, static_argnames=("seq_tile",))
def _forward(x, meta_tokens, norm_mix_pre, w_in, conv_w, pool_w, pool_scale,
             w_out, norm_mix_post, norm_ffn_pre, w_gate, w_up, w_down,
             norm_ffn_post, seq_tile=SEQ_TILE):
    bsz, seq, d_model = x.shape
    depth, _, d_in_proj = w_in.shape
    d_conv = d_in_proj // 4
    d_ff = w_gate.shape[2]
    assert depth == 1, "single block only: meta-token outputs are never formed"
    assert meta_tokens.shape == (N_META, d_model)
    assert pool_w.shape[1] == len(POOL_WINDOWS) and pool_w.shape[2] == LANES
    assert seq % seq_tile == 0 and seq_tile % POOL_HIST == 0
    assert d_ff % MXU_COLS == 0

    g_mix_pre = norm_mix_pre.astype(_F32)
    win = w_in[0].astype(_BF16)
    row = lambda v: v.reshape(1, -1).astype(_F32)

    meta_cv, meta_p = pl.pallas_call(
        _meta_kernel,
        out_shape=(jax.ShapeDtypeStruct((CONV_HIST, d_conv), _F32),
                   jax.ShapeDtypeStruct((POOL_HIST, d_conv), _F32)),
        name="meta_carry",
    )(meta_tokens.astype(_F32), g_mix_pre, win)

    consts = (
        meta_cv, meta_p, g_mix_pre, win, conv_w[0].astype(_F32),
        _block_diag(pool_w[0]).astype(_BF16), row(pool_scale[0]),
        w_out[0].astype(_BF16), row(norm_mix_post[0]), row(norm_ffn_pre[0]),
        w_gate[0].astype(_BF16), w_up[0].astype(_BF16), w_down[0].astype(_BF16),
        row(norm_ffn_post[0]),
    )
    tile_spec = pl.BlockSpec((1, seq_tile, d_model), lambda b, s: (b, s, 0))
    return pl.pallas_call(
        _block_kernel,
        grid=(bsz, seq // seq_tile),
        in_specs=[tile_spec] + [_resident(c.shape) for c in consts],
        out_specs=tile_spec,
        out_shape=jax.ShapeDtypeStruct(x.shape, x.dtype),
        scratch_shapes=[pltpu.VMEM((CONV_HIST, d_conv), _F32),
                        pltpu.VMEM((POOL_HIST, d_conv), _F32)],
        compiler_params=pltpu.CompilerParams(
            dimension_semantics=("arbitrary", "arbitrary"),
            vmem_limit_bytes=VMEM_LIMIT_BYTES),
        name="block",
    )(x, *consts)


def kernel(x, meta_tokens, norm_mix_pre, w_in, conv_w, pool_w, pool_scale, w_out,
           norm_mix_post, norm_ffn_pre, w_gate, w_up, w_down, norm_ffn_post):
    return _forward(x, meta_tokens, norm_mix_pre, w_in, conv_w, pool_w, pool_scale,
                    w_out, norm_mix_post, norm_ffn_pre, w_gate, w_up, w_down,
                    norm_ffn_post)
```

```python
import functools

import jax
import jax.numpy as jnp
from jax import lax
from jax.experimental import pallas as pl
from jax.experimental.pallas import tpu as pltpu

RMS_EPS = 1e-6
POOL_WINDOWS = (2, 4, 8, 16)
N_META = 16
SUBLANES = 8
LANES = 128
MXU_COLS = 256
CONV_HIST = SUBLANES
POOL_HIST = 16
SEQ_TILE = 512
POST_ROWS = 32
VMEM_LIMIT_BYTES = 56 * 1024 * 1024

_BF16 = jnp.bfloat16
_F32 = jnp.float32


def _rms_norm(x, g):
    y = x * lax.rsqrt(jnp.mean(x * x, axis=-1, keepdims=True) + RMS_EPS)
    return y * g


def _dot(a, b):
    return jnp.dot(a, b, preferred_element_type=_F32)


def _project_in(x, g_ref, win_ref):
    d_conv = win_ref.shape[1] // 4
    z = _dot(_rms_norm(x, g_ref[...]).astype(_BF16), win_ref[...])
    b_gate = z[:, 0:d_conv]
    cv = z[:, d_conv:2 * d_conv] * z[:, 2 * d_conv:3 * d_conv]
    p = z[:, 3 * d_conv:]
    return b_gate, cv, p


def _meta_kernel(meta_ref, g_ref, win_ref, cv_ref, p_ref):
    _, cv, p = _project_in(meta_ref[...], g_ref, win_ref)
    cv_ref[...] = cv[N_META - CONV_HIST:]
    p_ref[...] = p[N_META - POOL_HIST:]


def _causal_conv(cv_hist, cv, w):
    u = jnp.concatenate([cv_hist, cv], axis=0)
    k = w.shape[0]
    y = w[0:1] * pltpu.roll(u, k - 1, 0)
    for i in range(1, k):
        shifted = pltpu.roll(u, k - 1 - i, 0) if i < k - 1 else u
        y = y + w[i:i + 1] * shifted
    return y[CONV_HIST:]


def _trailing_mean_minus_token(p_hist, p):
    group = p.shape[1] // len(POOL_WINDOWS)
    s = jnp.concatenate([p_hist, p], axis=0)
    means = []
    win = 1
    for target in POOL_WINDOWS:
        while win < target:
            s = s + pltpu.roll(s, win, 0)
            win *= 2
        means.append(s[:, 0:group] * (1.0 / target))
        s = s[:, group:]
    return jnp.concatenate(means, axis=1)[POOL_HIST:] - p


def _mix_conv_pool(b_gate, cv, p, w, cv_hist_ref, p_hist_ref):
    tile = cv.shape[0]
    y_conv = b_gate * _causal_conv(cv_hist_ref[...], cv, w["convw"][...])
    pooled = _trailing_mean_minus_token(p_hist_ref[...], p)
    cv_hist_ref[...] = cv[tile - CONV_HIST:]
    p_hist_ref[...] = p[tile - POOL_HIST:]
    return y_conv, pooled.astype(_BF16)


def _zero_after(v):
    bits = lax.bitcast_convert_type(v, jnp.uint32)
    return lax.bitcast_convert_type((bits >> 16) >> 16, _F32)


def _mix_post(x_ref, m_ref, w, h_ref, f_ref):
    g_post, g_pre = w["g_mix_post"][...], w["g_ffn_pre"][...]
    anchor = None
    for r in range(0, m_ref.shape[0], POST_ROWS):
        rows = slice(r, r + POST_ROWS)
        m = m_ref[rows, :]
        ms = jnp.mean(m * m, axis=-1, keepdims=True)
        if anchor is not None:
            ms = ms + anchor
        h = x_ref[0, rows, :] + m * lax.rsqrt(ms + RMS_EPS) * g_post
        rs = lax.rsqrt(jnp.mean(h * h, axis=-1, keepdims=True) + RMS_EPS)
        anchor = _zero_after(rs)
        h_ref[rows, :] = h
        f_ref[rows, :] = (h * rs * g_pre).astype(_BF16)


def _swiglu_chunks(f, w, acc, start, stop):
    for j in range(start, stop):
        cols = slice(j * MXU_COLS, (j + 1) * MXU_COLS)
        gate = _dot(f, w["wg"][:, cols])
        up = _dot(f, w["wu"][:, cols])
        act = (gate / (1.0 + jnp.exp(-gate)) * up).astype(_BF16)
        part = _dot(act, w["wd"][cols, :])
        acc = part if acc is None else acc + part
    return acc


_WEIGHT_NAMES = ("mcv", "mp", "g_mix_pre", "win", "convw", "poolw", "pscale",
                 "wout", "g_mix_post", "g_ffn_pre", "wg", "wu", "wd", "g_ffn_post")


def _block_kernel(tiles_per_seq, x_ref, *refs):
    w = dict(zip(_WEIGHT_NAMES, refs))
    o_ref, cv_hist_ref, p_hist_ref, h0_ref, f0_ref, h1_ref, f1_ref = refs[len(_WEIGHT_NAMES):]
    k = pl.program_id(0)

    @pl.when(k == 0)
    def _():
        h1_ref[...] = jnp.zeros_like(h1_ref)
        f1_ref[...] = jnp.zeros_like(f1_ref)

    @pl.when(k % tiles_per_seq == 0)
    def _():
        cv_hist_ref[...] = w["mcv"][...]
        p_hist_ref[...] = w["mp"][...]

    n_chunks = w["wg"].shape[1] // MXU_COLS

    def step(h_prev_ref, f_prev_ref, h_next_ref, f_next_ref):
        x = x_ref[0]
        f_prev = f_prev_ref[...]
        acc = _swiglu_chunks(f_prev, w, None, 0, 2)
        b_gate, cv, p = _project_in(x, w["g_mix_pre"], w["win"])
        acc = _swiglu_chunks(f_prev, w, acc, 2, 5)
        y_conv, pooled = _mix_conv_pool(b_gate, cv, p, w, cv_hist_ref, p_hist_ref)
        y_pool = _dot(pooled, w["poolw"][...]) * w["pscale"][...]
        acc = _swiglu_chunks(f_prev, w, acc, 5, 7)
        mix_in = jnp.concatenate([y_conv, y_pool], axis=1).astype(_BF16)
        h_next_ref[...] = _dot(mix_in, w["wout"][...])
        acc = _swiglu_chunks(f_prev, w, acc, 7, 9)
        _mix_post(x_ref, h_next_ref, w, h_next_ref, f_next_ref)
        acc = _swiglu_chunks(f_prev, w, acc, 9, n_chunks)
        o_ref[0] = h_prev_ref[...] + _rms_norm(acc, w["g_ffn_post"][...])

    @pl.when(k % 2 == 0)
    def _():
        step(h1_ref, f1_ref, h0_ref, f0_ref)

    @pl.when(k % 2 == 1)
    def _():
        step(h0_ref, f0_ref, h1_ref, f1_ref)


def _resident(shape):
    return pl.BlockSpec(shape, lambda *_: (0,) * len(shape),
                        pipeline_mode=pl.Buffered(1))


def _block_diag(w):
    groups, n, _ = w.shape
    eye = jnp.eye(groups, dtype=w.dtype)
    return (eye[:, None, :, None] * w[:, :, None, :]).reshape(groups * n, groups * n)


@functools.partial(jax.jit, static_argnames=("seq_tile",))
def _forward(x, meta_tokens, norm_mix_pre, w_in, conv_w, pool_w, pool_scale,
             w_out, norm_mix_post, norm_ffn_pre, w_gate, w_up, w_down,
             norm_ffn_post, seq_tile=SEQ_TILE):
    bsz, seq, d_model = x.shape
    depth, _, d_in_proj = w_in.shape
    d_conv = d_in_proj // 4
    d_ff = w_gate.shape[2]
    assert depth == 1, "single block only: meta-token outputs are never formed"
    assert meta_tokens.shape == (N_META, d_model)
    assert pool_w.shape[1] == len(POOL_WINDOWS) and pool_w.shape[2] == LANES
    assert pool_scale.shape[1] == d_conv
    assert seq % seq_tile == 0 and seq_tile % POOL_HIST == 0
    assert d_ff % MXU_COLS == 0

    g_mix_pre = norm_mix_pre.astype(_F32)
    win = w_in[0].astype(_BF16)
    row = lambda v: v.reshape(1, -1).astype(_F32)

    meta_cv, meta_p = pl.pallas_call(
        _meta_kernel,
        out_shape=(jax.ShapeDtypeStruct((CONV_HIST, d_conv), _F32),
                   jax.ShapeDtypeStruct((POOL_HIST, d_conv), _F32)),
        name="meta_carry",
    )(meta_tokens.astype(_F32), g_mix_pre, win)

    consts = (
        meta_cv, meta_p, g_mix_pre, win, conv_w[0].astype(_F32),
        _block_diag(pool_w[0]).astype(_BF16), row(pool_scale[0]),
        w_out[0].astype(_BF16), row(norm_mix_post[0]), row(norm_ffn_pre[0]),
        w_gate[0].astype(_BF16), w_up[0].astype(_BF16), w_down[0].astype(_BF16),
        row(norm_ffn_post[0]),
    )
    assert len(consts) == len(_WEIGHT_NAMES)

    tiles_per_seq = seq // seq_tile
    n_tiles = bsz * tiles_per_seq

    def tile_index(t):
        return (t // tiles_per_seq, t % tiles_per_seq, 0)

    x_spec = pl.BlockSpec((1, seq_tile, d_model),
                          lambda k: tile_index(jnp.minimum(k, n_tiles - 1)))
    o_spec = pl.BlockSpec((1, seq_tile, d_model),
                          lambda k: tile_index(jnp.maximum(k - 1, 0)))
    slot = [pltpu.VMEM((seq_tile, d_model), _F32), pltpu.VMEM((seq_tile, d_model), _BF16)]
    return pl.pallas_call(
        functools.partial(_block_kernel, tiles_per_seq),
        grid=(n_tiles + 1,),
        in_specs=[x_spec] + [_resident(c.shape) for c in consts],
        out_specs=o_spec,
        out_shape=jax.ShapeDtypeStruct(x.shape, x.dtype),
        scratch_shapes=[pltpu.VMEM((CONV_HIST, d_conv), _F32),
                        pltpu.VMEM((POOL_HIST, d_conv), _F32)] + slot + slot,
        compiler_params=pltpu.CompilerParams(
            dimension_semantics=("arbitrary",),
            vmem_limit_bytes=VMEM_LIMIT_BYTES),
        name="block",
    )(x, *consts)


def kernel(x, meta_tokens, norm_mix_pre, w_in, conv_w, pool_w, pool_scale, w_out,
           norm_mix_post, norm_ffn_pre, w_gate, w_up, w_down, norm_ffn_post):
    return _forward(x, meta_tokens, norm_mix_pre, w_in, conv_w, pool_w, pool_scale,
                    w_out, norm_mix_post, norm_ffn_pre, w_gate, w_up, w_down,
                    norm_ffn_post)
```

```python
import functools

import jax
import jax.numpy as jnp
from jax import lax
from jax.experimental import pallas as pl
from jax.experimental.pallas import tpu as pltpu

RMS_EPS = 1e-6
POOL_WINDOWS = (2, 4, 8, 16)
N_META = 16
SUBLANES = 8
LANES = 128
MXU_COLS = 256
CONV_HIST = SUBLANES
POOL_HIST = 16
SEQ_TILE = 512
POST_ROWS = 16
MIX_IN_LO_AT, MIX_IN_HI_AT, MIX_POOL_AT, MIX_OUT_AT, MIX_POST_AT = 2, 3, 5, 7, 9
DOWN_GROUP = 3
DOWN_LAG = 2
VMEM_LIMIT_BYTES = 56 * 1024 * 1024

_BF16 = jnp.bfloat16
_F32 = jnp.float32


def _rms_norm(x, g):
    y = x * lax.rsqrt(jnp.mean(x * x, axis=-1, keepdims=True) + RMS_EPS)
    return y * g


def _dot(a, b):
    return jnp.dot(a, b, preferred_element_type=_F32)


def _project_in_half(a, win_ref, half):
    n = win_ref.shape[1] // 2
    return _dot(a, win_ref[:, half * n:(half + 1) * n])


def _split_projection(z_lo, z_hi):
    d_conv = z_lo.shape[1] // 2
    return z_lo[:, :d_conv], z_lo[:, d_conv:] * z_hi[:, :d_conv], z_hi[:, d_conv:]


def _project_in(x, g_ref, win_ref):
    a = _rms_norm(x, g_ref[...]).astype(_BF16)
    return _split_projection(_project_in_half(a, win_ref, 0), _project_in_half(a, win_ref, 1))


def _meta_kernel(meta_ref, g_ref, win_ref, cv_ref, p_ref):
    _, cv, p = _project_in(meta_ref[...], g_ref, win_ref)
    cv_ref[...] = cv[N_META - CONV_HIST:]
    p_ref[...] = p[N_META - POOL_HIST:]


def _causal_conv(cv_hist, cv, w):
    u = jnp.concatenate([cv_hist, cv], axis=0)
    k = w.shape[0]
    y = w[0:1] * pltpu.roll(u, k - 1, 0)
    for i in range(1, k):
        shifted = pltpu.roll(u, k - 1 - i, 0) if i < k - 1 else u
        y = y + w[i:i + 1] * shifted
    return y[CONV_HIST:]


def _trailing_mean_minus_token(p_hist, p):
    group = p.shape[1] // len(POOL_WINDOWS)
    s = jnp.concatenate([p_hist, p], axis=0)
    means = []
    win = 1
    for target in POOL_WINDOWS:
        while win < target:
            s = s + pltpu.roll(s, win, 0)
            win *= 2
        means.append(s[:, 0:group] * (1.0 / target))
        s = s[:, group:]
    return jnp.concatenate(means, axis=1)[POOL_HIST:] - p


def _mix_conv_pool(b_gate, cv, p, w, cv_hist_ref, p_hist_ref):
    tile = cv.shape[0]
    y_conv = b_gate * _causal_conv(cv_hist_ref[...], cv, w["convw"][...])
    pooled = _trailing_mean_minus_token(p_hist_ref[...], p)
    cv_hist_ref[...] = cv[tile - CONV_HIST:]
    p_hist_ref[...] = p[tile - POOL_HIST:]
    return y_conv, pooled.astype(_BF16)


def _zero_after(v):
    bits = lax.bitcast_convert_type(v, jnp.uint32)
    return lax.bitcast_convert_type((bits >> 16) >> 16, _F32)


def _mix_post(x_ref, m_ref, w, h_ref, f_ref):
    g_post, g_pre = w["g_mix_post"][...], w["g_ffn_pre"][...]
    anchor = None
    for r in range(0, m_ref.shape[0], POST_ROWS):
        rows = slice(r, r + POST_ROWS)
        m = m_ref[rows, :]
        ms = jnp.mean(m * m, axis=-1, keepdims=True)
        if anchor is not None:
            ms = ms + anchor
        h = x_ref[0, rows, :] + m * lax.rsqrt(ms + RMS_EPS) * g_post
        rs = lax.rsqrt(jnp.mean(h * h, axis=-1, keepdims=True) + RMS_EPS)
        anchor = _zero_after(rs)
        h_ref[rows, :] = h
        f_ref[rows, :] = (h * rs * g_pre).astype(_BF16)


def _swiglu_act(f, w, j):
    gu = _dot(f, w["wgu"][:, 2 * j * MXU_COLS:2 * (j + 1) * MXU_COLS])
    gate, up = gu[:, :MXU_COLS], gu[:, MXU_COLS:]
    return (gate / (1.0 + jnp.exp(-gate)) * up).astype(_BF16)


def _swiglu_down(acts, w, j, acc):
    rows = slice(j * MXU_COLS, (j + len(acts)) * MXU_COLS)
    part = _dot(jnp.concatenate(acts, axis=1), w["wd"][rows, :])
    return part if acc is None else acc + part


_WEIGHT_NAMES = ("mcv", "mp", "g_mix_pre", "win", "convw", "poolw", "pscale",
                 "wout", "g_mix_post", "g_ffn_pre", "wgu", "wd", "g_ffn_post")


def _block_kernel(tiles_per_seq, x_ref, *refs):
    w = dict(zip(_WEIGHT_NAMES, refs))
    o_ref, cv_hist_ref, p_hist_ref, h0_ref, f0_ref, h1_ref, f1_ref = refs[len(_WEIGHT_NAMES):]
    k = pl.program_id(0)

    @pl.when(k == 0)
    def _():
        h1_ref[...] = jnp.zeros_like(h1_ref)
        f1_ref[...] = jnp.zeros_like(f1_ref)

    @pl.when(k % tiles_per_seq == 0)
    def _():
        cv_hist_ref[...] = w["mcv"][...]
        p_hist_ref[...] = w["mp"][...]

    n_chunks = w["wd"].shape[0] // MXU_COLS

    def step(h_prev_ref, f_prev_ref, h_next_ref, f_next_ref):
        x = x_ref[0]
        f_prev = f_prev_ref[...]
        mix = {}

        def mix_in_lo():
            mix["a"] = _rms_norm(x, w["g_mix_pre"][...]).astype(_BF16)
            mix["z_lo"] = _project_in_half(mix["a"], w["win"], 0)

        def mix_in_hi():
            mix["z_hi"] = _project_in_half(mix.pop("a"), w["win"], 1)

        def mix_pool():
            b_gate, cv, p = _split_projection(mix.pop("z_lo"), mix.pop("z_hi"))
            mix["y_conv"], pooled = _mix_conv_pool(b_gate, cv, p, w, cv_hist_ref, p_hist_ref)
            mix["y_pool"] = _dot(pooled, w["poolw"][...]) * w["pscale"][...]

        def mix_out_proj():
            mix_in = jnp.concatenate([mix.pop("y_conv"), mix.pop("y_pool")], axis=1)
            h_next_ref[...] = _dot(mix_in.astype(_BF16), w["wout"][...])

        def mix_post():
            _mix_post(x_ref, h_next_ref, w, h_next_ref, f_next_ref)

        before_chunk = {MIX_IN_LO_AT: mix_in_lo, MIX_IN_HI_AT: mix_in_hi, MIX_POOL_AT: mix_pool,
                        MIX_OUT_AT: mix_out_proj, MIX_POST_AT: mix_post}
        acc = None
        acts = []
        first = 0
        for j in range(n_chunks):
            if j in before_chunk:
                before_chunk[j]()
            acts.append(_swiglu_act(f_prev, w, j))
            if len(acts) == DOWN_GROUP + DOWN_LAG:
                acc = _swiglu_down(acts[:DOWN_GROUP], w, first, acc)
                del acts[:DOWN_GROUP]
                first += DOWN_GROUP
        while acts:
            acc = _swiglu_down(acts[:DOWN_GROUP], w, first, acc)
            del acts[:DOWN_GROUP]
            first += DOWN_GROUP
        o_ref[0] = h_prev_ref[...] + _rms_norm(acc, w["g_ffn_post"][...])

    @pl.when(k % 2 == 0)
    def _():
        step(h1_ref, f1_ref, h0_ref, f0_ref)

    @pl.when(k % 2 == 1)
    def _():
        step(h0_ref, f0_ref, h1_ref, f1_ref)


def _resident(shape):
    return pl.BlockSpec(shape, lambda *_: (0,) * len(shape),
                        pipeline_mode=pl.Buffered(1))


def _block_diag(w):
    groups, n, _ = w.shape
    eye = jnp.eye(groups, dtype=w.dtype)
    return (eye[:, None, :, None] * w[:, :, None, :]).reshape(groups * n, groups * n)


def _interleave_chunks(a, b):
    d, n = a.shape
    pair = jnp.stack([a.reshape(d, n // MXU_COLS, MXU_COLS),
                      b.reshape(d, n // MXU_COLS, MXU_COLS)], axis=2)
    return pair.reshape(d, 2 * n)


@functools.partial(jax.jit, static_argnames=("seq_tile",))
def _forward(x, meta_tokens, norm_mix_pre, w_in, conv_w, pool_w, pool_scale,
             w_out, norm_mix_post, norm_ffn_pre, w_gate, w_up, w_down,
             norm_ffn_post, seq_tile=SEQ_TILE):
    bsz, seq, d_model = x.shape
    depth, _, d_in_proj = w_in.shape
    d_conv = d_in_proj // 4
    d_ff = w_gate.shape[2]
    assert depth == 1, "single block only: meta-token outputs are never formed"
    assert meta_tokens.shape == (N_META, d_model)
    assert pool_w.shape[1] == len(POOL_WINDOWS) and pool_w.shape[2] == LANES
    assert pool_scale.shape[1] == d_conv
    assert seq % seq_tile == 0 and seq_tile % POOL_HIST == 0
    assert d_ff % MXU_COLS == 0

    g_mix_pre = norm_mix_pre.astype(_F32)
    win = w_in[0].astype(_BF16)
    row = lambda v: v.reshape(1, -1).astype(_F32)

    meta_cv, meta_p = pl.pallas_call(
        _meta_kernel,
        out_shape=(jax.ShapeDtypeStruct((CONV_HIST, d_conv), _F32),
                   jax.ShapeDtypeStruct((POOL_HIST, d_conv), _F32)),
        name="meta_carry",
    )(meta_tokens.astype(_F32), g_mix_pre, win)

    consts = (
        meta_cv, meta_p, g_mix_pre, win, conv_w[0].astype(_F32),
        _block_diag(pool_w[0]).astype(_BF16), row(pool_scale[0]),
        w_out[0].astype(_BF16), row(norm_mix_post[0]), row(norm_ffn_pre[0]),
        _interleave_chunks(w_gate[0], w_up[0]).astype(_BF16), w_down[0].astype(_BF16),
        row(norm_ffn_post[0]),
    )
    assert len(consts) == len(_WEIGHT_NAMES)

    tiles_per_seq = seq // seq_tile
    n_tiles = bsz * tiles_per_seq

    def tile_index(t):
        return (t // tiles_per_seq, t % tiles_per_seq, 0)

    x_spec = pl.BlockSpec((1, seq_tile, d_model),
                          lambda k: tile_index(jnp.minimum(k, n_tiles - 1)))
    o_spec = pl.BlockSpec((1, seq_tile, d_model),
                          lambda k: tile_index(jnp.maximum(k - 1, 0)))
    slot = [pltpu.VMEM((seq_tile, d_model), _F32), pltpu.VMEM((seq_tile, d_model), _BF16)]
    return pl.pallas_call(
        functools.partial(_block_kernel, tiles_per_seq),
        grid=(n_tiles + 1,),
        in_specs=[x_spec] + [_resident(c.shape) for c in consts],
        out_specs=o_spec,
        out_shape=jax.ShapeDtypeStruct(x.shape, x.dtype),
        scratch_shapes=[pltpu.VMEM((CONV_HIST, d_conv), _F32),
                        pltpu.VMEM((POOL_HIST, d_conv), _F32)] + slot + slot,
        compiler_params=pltpu.CompilerParams(
            dimension_semantics=("arbitrary",),
            vmem_limit_bytes=VMEM_LIMIT_BYTES),
        name="block",
    )(x, *consts)


def kernel(x, meta_tokens, norm_mix_pre, w_in, conv_w, pool_w, pool_scale, w_out,
           norm_mix_post, norm_ffn_pre, w_gate, w_up, w_down, norm_ffn_post):
    return _forward(x, meta_tokens, norm_mix_pre, w_in, conv_w, pool_w, pool_scale,
                    w_out, norm_mix_post, norm_ffn_pre, w_gate, w_up, w_down,
                    norm_ffn_post)
```

```python
import functools

import jax
import jax.numpy as jnp
from jax import lax
from jax.experimental import pallas as pl
from jax.experimental.pallas import tpu as pltpu

RMS_EPS = 1e-6
POOL_WINDOWS = (2, 4, 8, 16)
N_META = 16
SUBLANES = 8
LANES = 128
MXU_COLS = 256
CONV_HIST = SUBLANES
POOL_HIST = 16
SEQ_TILE = 512
POST_ROWS = 16
MIX_IN_LO_AT, MIX_IN_HI_AT, MIX_POOL_AT, MIX_OUT_AT, MIX_POST_AT = 2, 3, 5, 7, 9
DOWN_GROUP = 3
DOWN_LAG = 2
VMEM_LIMIT_BYTES = 56 * 1024 * 1024

_BF16 = jnp.bfloat16
_F32 = jnp.float32


def _rms_norm(x, g):
    y = x * lax.rsqrt(jnp.mean(x * x, axis=-1, keepdims=True) + RMS_EPS)
    return y * g


def _dot(a, b):
    return jnp.dot(a, b, preferred_element_type=_F32)


def _project_in_half(a, win_ref, half):
    n = win_ref.shape[1] // 2
    return _dot(a, win_ref[:, half * n:(half + 1) * n])


def _split_projection(z_lo, z_hi):
    d_conv = z_lo.shape[1] // 2
    return z_lo[:, :d_conv], z_lo[:, d_conv:] * z_hi[:, :d_conv], z_hi[:, d_conv:]


def _project_in(x, g_ref, win_ref):
    a = _rms_norm(x, g_ref[...]).astype(_BF16)
    return _split_projection(_project_in_half(a, win_ref, 0), _project_in_half(a, win_ref, 1))


def _meta_kernel(meta_ref, g_ref, win_ref, cv_ref, p_ref):
    _, cv, p = _project_in(meta_ref[...], g_ref, win_ref)
    cv_ref[...] = cv[N_META - CONV_HIST:]
    p_ref[...] = p[N_META - POOL_HIST:]


def _causal_conv(cv_hist, cv, w):
    u = jnp.concatenate([cv_hist, cv], axis=0)
    k = w.shape[0]
    y = w[0:1] * pltpu.roll(u, k - 1, 0)
    for i in range(1, k):
        shifted = pltpu.roll(u, k - 1 - i, 0) if i < k - 1 else u
        y = y + w[i:i + 1] * shifted
    return y[CONV_HIST:]


def _trailing_mean_minus_token(p_hist, p):
    group = p.shape[1] // len(POOL_WINDOWS)
    s = jnp.concatenate([p_hist, p], axis=0)
    means = []
    win = 1
    for target in POOL_WINDOWS:
        while win < target:
            s = s + pltpu.roll(s, win, 0)
            win *= 2
        means.append(s[:, 0:group] * (1.0 / target))
        s = s[:, group:]
    return jnp.concatenate(means, axis=1)[POOL_HIST:] - p


def _mix_conv_pool(b_gate, cv, p, w, cv_hist_ref, p_hist_ref):
    tile = cv.shape[0]
    y_conv = b_gate * _causal_conv(cv_hist_ref[...], cv, w["convw"][...])
    pooled = _trailing_mean_minus_token(p_hist_ref[...], p)
    cv_hist_ref[...] = cv[tile - CONV_HIST:]
    p_hist_ref[...] = p[tile - POOL_HIST:]
    return y_conv, pooled.astype(_BF16)


def _zero_after(v):
    bits = lax.bitcast_convert_type(v, jnp.uint32)
    return lax.bitcast_convert_type((bits >> 16) >> 16, _F32)


def _mix_post(x_ref, m_ref, w, h_ref, f_ref):
    g_post, g_pre = w["g_mix_post"][...], w["g_ffn_pre"][...]
    anchor = None
    for r in range(0, m_ref.shape[0], POST_ROWS):
        rows = slice(r, r + POST_ROWS)
        m = m_ref[rows, :]
        ms = jnp.mean(m * m, axis=-1, keepdims=True)
        if anchor is not None:
            ms = ms + anchor
        h = x_ref[0, rows, :] + m * lax.rsqrt(ms + RMS_EPS) * g_post
        rs = lax.rsqrt(jnp.mean(h * h, axis=-1, keepdims=True) + RMS_EPS)
        anchor = _zero_after(rs)
        h_ref[rows, :] = h
        f_ref[rows, :] = (h * rs * g_pre).astype(_BF16)


def _swiglu_act(f, w, j):
    cols = slice(j * MXU_COLS, (j + 1) * MXU_COLS)
    gate = _dot(f, w["wg"][:, cols])
    up = _dot(f, w["wu"][:, cols])
    return (gate / (1.0 + jnp.exp(-gate)) * up).astype(_BF16)


def _swiglu_down(acts, w, j, acc):
    rows = slice(j * MXU_COLS, (j + len(acts)) * MXU_COLS)
    part = _dot(jnp.concatenate(acts, axis=1), w["wd"][rows, :])
    return part if acc is None else acc + part


_WEIGHT_NAMES = ("mcv", "mp", "g_mix_pre", "win", "convw", "poolw", "pscale",
                 "wout", "g_mix_post", "g_ffn_pre", "wg", "wu", "wd", "g_ffn_post")


def _block_kernel(tiles_per_seq, x_ref, *refs):
    w = dict(zip(_WEIGHT_NAMES, refs))
    o_ref, cv_hist_ref, p_hist_ref, h0_ref, f0_ref, h1_ref, f1_ref = refs[len(_WEIGHT_NAMES):]
    k = pl.program_id(0)

    @pl.when(k == 0)
    def _():
        h1_ref[...] = jnp.zeros_like(h1_ref)
        f1_ref[...] = jnp.zeros_like(f1_ref)

    @pl.when(k % tiles_per_seq == 0)
    def _():
        cv_hist_ref[...] = w["mcv"][...]
        p_hist_ref[...] = w["mp"][...]

    n_chunks = w["wd"].shape[0] // MXU_COLS

    def step(h_prev_ref, f_prev_ref, h_next_ref, f_next_ref):
        x = x_ref[0]
        f_prev = f_prev_ref[...]
        mix = {}

        def mix_in_lo():
            mix["a"] = _rms_norm(x, w["g_mix_pre"][...]).astype(_BF16)
            mix["z_lo"] = _project_in_half(mix["a"], w["win"], 0)

        def mix_in_hi():
            mix["z_hi"] = _project_in_half(mix.pop("a"), w["win"], 1)

        def mix_pool():
            b_gate, cv, p = _split_projection(mix.pop("z_lo"), mix.pop("z_hi"))
            mix["y_conv"], pooled = _mix_conv_pool(b_gate, cv, p, w, cv_hist_ref, p_hist_ref)
            mix["y_pool"] = _dot(pooled, w["poolw"][...]) * w["pscale"][...]

        def mix_out_proj():
            mix_in = jnp.concatenate([mix.pop("y_conv"), mix.pop("y_pool")], axis=1)
            h_next_ref[...] = _dot(mix_in.astype(_BF16), w["wout"][...])

        def mix_post():
            _mix_post(x_ref, h_next_ref, w, h_next_ref, f_next_ref)

        before_chunk = {MIX_IN_LO_AT: mix_in_lo, MIX_IN_HI_AT: mix_in_hi, MIX_POOL_AT: mix_pool,
                        MIX_OUT_AT: mix_out_proj, MIX_POST_AT: mix_post}
        acc = None
        acts = []
        first = 0
        for j in range(n_chunks):
            if j in before_chunk:
                before_chunk[j]()
            acts.append(_swiglu_act(f_prev, w, j))
            if len(acts) == DOWN_GROUP + DOWN_LAG:
                acc = _swiglu_down(acts[:DOWN_GROUP], w, first, acc)
                del acts[:DOWN_GROUP]
                first += DOWN_GROUP
        while acts:
            acc = _swiglu_down(acts[:DOWN_GROUP], w, first, acc)
            del acts[:DOWN_GROUP]
            first += DOWN_GROUP
        o_ref[0] = h_prev_ref[...] + _rms_norm(acc, w["g_ffn_post"][...])

    @pl.when(k % 2 == 0)
    def _():
        step(h1_ref, f1_ref, h0_ref, f0_ref)

    @pl.when(k % 2 == 1)
    def _():
        step(h0_ref, f0_ref, h1_ref, f1_ref)


def _resident(shape):
    return pl.BlockSpec(shape, lambda *_: (0,) * len(shape),
                        pipeline_mode=pl.Buffered(1))


def _block_diag(w):
    groups, n, _ = w.shape
    eye = jnp.eye(groups, dtype=w.dtype)
    return (eye[:, None, :, None] * w[:, :, None, :]).reshape(groups * n, groups * n)


@functools.partial(jax.jit, static_argnames=("seq_tile",))
def _forward(x, meta_tokens, norm_mix_pre, w_in, conv_w, pool_w, pool_scale,
             w_out, norm_mix_post, norm_ffn_pre, w_gate, w_up, w_down,
             norm_ffn_post, seq_tile=SEQ_TILE):
    bsz, seq, d_model = x.shape
    depth, _, d_in_proj = w_in.shape
    d_conv = d_in_proj // 4
    d_ff = w_gate.shape[2]
    assert depth == 1, "single block only: meta-token outputs are never formed"
    assert meta_tokens.shape == (N_META, d_model)
    assert pool_w.shape[1] == len(POOL_WINDOWS) and pool_w.shape[2] == LANES
    assert pool_scale.shape[1] == d_conv
    assert seq % seq_tile == 0 and seq_tile % POOL_HIST == 0
    assert d_ff % MXU_COLS == 0

    g_mix_pre = norm_mix_pre.astype(_F32)
    win = w_in[0].astype(_BF16)
    row = lambda v: v.reshape(1, -1).astype(_F32)

    meta_cv, meta_p = pl.pallas_call(
        _meta_kernel,
        out_shape=(jax.ShapeDtypeStruct((CONV_HIST, d_conv), _F32),
                   jax.ShapeDtypeStruct((POOL_HIST, d_conv), _F32)),
        name="meta_carry",
    )(meta_tokens.astype(_F32), g_mix_pre, win)

    consts = (
        meta_cv, meta_p, g_mix_pre, win, conv_w[0].astype(_F32),
        _block_diag(pool_w[0]).astype(_BF16), row(pool_scale[0]),
        w_out[0].astype(_BF16), row(norm_mix_post[0]), row(norm_ffn_pre[0]),
        w_gate[0].astype(_BF16), w_up[0].astype(_BF16), w_down[0].astype(_BF16),
        row(norm_ffn_post[0]),
    )
    assert len(consts) == len(_WEIGHT_NAMES)

    tiles_per_seq = seq // seq_tile
    n_tiles = bsz * tiles_per_seq

    def tile_index(t):
        return (t // tiles_per_seq, t % tiles_per_seq, 0)

    x_spec = pl.BlockSpec((1, seq_tile, d_model),
                          lambda k: tile_index(jnp.minimum(k, n_tiles - 1)))
    o_spec = pl.BlockSpec((1, seq_tile, d_model),
                          lambda k: tile_index(jnp.maximum(k - 1, 0)))
    slot = [pltpu.VMEM((seq_tile, d_model), _F32), pltpu.VMEM((seq_tile, d_model), _BF16)]
    return pl.pallas_call(
        functools.partial(_block_kernel, tiles_per_seq),
        grid=(n_tiles + 1,),
        in_specs=[x_spec] + [_resident(c.shape) for c in consts],
        out_specs=o_spec,
        out_shape=jax.ShapeDtypeStruct(x.shape, x.dtype),
        scratch_shapes=[pltpu.VMEM((CONV_HIST, d_conv), _F32),
                        pltpu.VMEM((POOL_HIST, d_conv), _F32)] + slot + slot,
        compiler_params=pltpu.CompilerParams(
            dimension_semantics=("arbitrary",),
            vmem_limit_bytes=VMEM_LIMIT_BYTES),
        name="block",
    )(x, *consts)


def kernel(x, meta_tokens, norm_mix_pre, w_in, conv_w, pool_w, pool_scale, w_out,
           norm_mix_post, norm_ffn_pre, w_gate, w_up, w_down, norm_ffn_post):
    return _forward(x, meta_tokens, norm_mix_pre, w_in, conv_w, pool_w, pool_scale,
                    w_out, norm_mix_post, norm_ffn_pre, w_gate, w_up, w_down,
                    norm_ffn_post)
```

```python
import functools

import jax
import jax.numpy as jnp
from jax import lax
from jax.experimental import pallas as pl
from jax.experimental.pallas import tpu as pltpu

RMS_EPS = 1e-6
POOL_WINDOWS = (2, 4, 8, 16)
N_META = 16
SUBLANES = 8
LANES = 128
MXU_COLS = 256
CONV_HIST = SUBLANES
POOL_HIST = 16
SEQ_TILE = 512
MIX_ROWS = 32
POST_ROWS = 16
MIX_IN_LO_AT, MIX_IN_HI_AT, MIX_POOL_AT, MIX_OUT_AT, MIX_POST_AT = 2, 3, 6, 7, 9
DOWN_GROUP = 3
DOWN_LAG = 2
VMEM_LIMIT_BYTES = 56 * 1024 * 1024

_BF16 = jnp.bfloat16
_F32 = jnp.float32


def _rms_norm(x, g):
    y = x * lax.rsqrt(jnp.mean(x * x, axis=-1, keepdims=True) + RMS_EPS)
    return y * g


def _dot(a, b):
    return jnp.dot(a, b, preferred_element_type=_F32)


def _project_in_half(a, win_ref, half):
    n = win_ref.shape[1] // 2
    return _dot(a, win_ref[:, half * n:(half + 1) * n])


def _split_projection(z_lo, z_hi):
    d_conv = z_lo.shape[1] // 2
    return z_lo[:, :d_conv], z_lo[:, d_conv:] * z_hi[:, :d_conv], z_hi[:, d_conv:]


def _meta_kernel(meta_ref, g_ref, win_ref, cv_ref, p_ref):
    a = _rms_norm(meta_ref[...], g_ref[...]).astype(_BF16)
    _, cv, p = _split_projection(_project_in_half(a, win_ref, 0), _project_in_half(a, win_ref, 1))
    cv_ref[...] = cv[N_META - CONV_HIST:]
    p_ref[...] = p[N_META - POOL_HIST:]


def _zero_after(v):
    bits = lax.bitcast_convert_type(v, jnp.uint32)
    return lax.bitcast_convert_type((bits >> 16) >> 16, _F32)


def _lane_groups(n_lanes):
    return [slice(g * LANES, (g + 1) * LANES) for g in range(n_lanes // LANES)]


def _stage_rows(buf_ref, hist, values):
    tile = values.shape[0]
    for g, lanes in enumerate(_lane_groups(values.shape[1])):
        buf_ref[g, 0:hist, :] = buf_ref[g, tile:tile + hist, :]
        buf_ref[g, hist:hist + tile, :] = values[:, lanes]


def _conv_pool_rows(b_gate, cv_ref, p_ref, conv_w, mix_ref, pooled_ref):
    tile = b_gate.shape[0]
    taps = conv_w.shape[0]
    anchor = None
    for r in range(0, tile, MIX_ROWS):
        rows = slice(r, r + MIX_ROWS)
        last = None
        for g, lanes in enumerate(_lane_groups(b_gate.shape[1])):
            w_g = conv_w[:, lanes]
            if anchor is not None:
                w_g = w_g + anchor
            y = None
            for i in range(taps):
                u = cv_ref[g, pl.ds(CONV_HIST + r - (taps - 1 - i), MIX_ROWS), :]
                y = w_g[i:i + 1] * u if y is None else y + w_g[i:i + 1] * u
            mix_ref[rows, lanes] = (b_gate[rows, lanes] * y).astype(_BF16)

            win = POOL_WINDOWS[g]
            s = p_ref[g, pl.ds(POOL_HIST + r, MIX_ROWS), :]
            token = s
            for j in range(1, win):
                s = s + p_ref[g, pl.ds(POOL_HIST + r - j, MIX_ROWS), :]
            inv = jnp.full((1, LANES), 1.0 / win, _F32)
            if anchor is not None:
                inv = inv + anchor
            last = s * inv - token
            pooled_ref[rows, lanes] = last.astype(_BF16)
        anchor = _zero_after(pltpu.roll(last[0:SUBLANES], 1, 1))[0:1]


def _mix_post(x_ref, m_ref, w, h_ref, f_ref):
    g_post, g_pre = w["g_mix_post"][...], w["g_ffn_pre"][...]
    anchor = None
    for r in range(0, m_ref.shape[0], POST_ROWS):
        rows = slice(r, r + POST_ROWS)
        m = m_ref[rows, :]
        ms = jnp.mean(m * m, axis=-1, keepdims=True)
        if anchor is not None:
            ms = ms + anchor
        h = x_ref[0, rows, :] + m * lax.rsqrt(ms + RMS_EPS) * g_post
        rs = lax.rsqrt(jnp.mean(h * h, axis=-1, keepdims=True) + RMS_EPS)
        anchor = _zero_after(rs)
        h_ref[rows, :] = h
        f_ref[rows, :] = (h * rs * g_pre).astype(_BF16)


def _swiglu_act(f, w, j):
    cols = slice(j * MXU_COLS, (j + 1) * MXU_COLS)
    gate = _dot(f, w["wg"][:, cols])
    up = _dot(f, w["wu"][:, cols])
    return (gate / (1.0 + jnp.exp(-gate)) * up).astype(_BF16)


def _swiglu_down(acts, w, j, acc):
    rows = slice(j * MXU_COLS, (j + len(acts)) * MXU_COLS)
    part = _dot(jnp.concatenate(acts, axis=1), w["wd"][rows, :])
    return part if acc is None else acc + part


_WEIGHT_NAMES = ("mcv", "mp", "g_mix_pre", "win", "convw", "poolw", "pscale",
                 "wout", "g_mix_post", "g_ffn_pre", "wg", "wu", "wd", "g_ffn_post")


def _block_kernel(tiles_per_seq, x_ref, *refs):
    w = dict(zip(_WEIGHT_NAMES, refs))
    (o_ref, cv_ref, p_ref, mix_ref, pooled_ref,
     h0_ref, f0_ref, h1_ref, f1_ref) = refs[len(_WEIGHT_NAMES):]
    k = pl.program_id(0)
    tile = x_ref.shape[1]
    d_conv = mix_ref.shape[1] // 2

    @pl.when(k == 0)
    def _():
        h1_ref[...] = jnp.zeros_like(h1_ref)
        f1_ref[...] = jnp.zeros_like(f1_ref)

    @pl.when(k % tiles_per_seq == 0)
    def _():
        for g, lanes in enumerate(_lane_groups(d_conv)):
            cv_ref[g, tile:tile + CONV_HIST, :] = w["mcv"][:, lanes]
            p_ref[g, tile:tile + POOL_HIST, :] = w["mp"][:, lanes]

    n_chunks = w["wd"].shape[0] // MXU_COLS

    def step(h_prev_ref, f_prev_ref, h_next_ref, f_next_ref):
        x = x_ref[0]
        f_prev = f_prev_ref[...]
        mix = {}

        def mix_in_lo():
            mix["a"] = _rms_norm(x, w["g_mix_pre"][...]).astype(_BF16)
            mix["z_lo"] = _project_in_half(mix["a"], w["win"], 0)

        def mix_in_hi():
            mix["z_hi"] = _project_in_half(mix.pop("a"), w["win"], 1)

        def mix_pool():
            b_gate, cv, p = _split_projection(mix.pop("z_lo"), mix.pop("z_hi"))
            _stage_rows(cv_ref, CONV_HIST, cv)
            _stage_rows(p_ref, POOL_HIST, p)
            _conv_pool_rows(b_gate, cv_ref, p_ref, w["convw"][...], mix_ref, pooled_ref)
            y_pool = _dot(pooled_ref[...], w["poolw"][...]) * w["pscale"][...]
            mix_ref[:, d_conv:] = y_pool.astype(_BF16)

        def mix_out_proj():
            h_next_ref[...] = _dot(mix_ref[...], w["wout"][...])

        def mix_post():
            _mix_post(x_ref, h_next_ref, w, h_next_ref, f_next_ref)

        before_chunk = {MIX_IN_LO_AT: mix_in_lo, MIX_IN_HI_AT: mix_in_hi, MIX_POOL_AT: mix_pool,
                        MIX_OUT_AT: mix_out_proj, MIX_POST_AT: mix_post}
        acc = None
        acts = []
        first = 0
        for j in range(n_chunks):
            if j in before_chunk:
                before_chunk[j]()
            acts.append(_swiglu_act(f_prev, w, j))
            if len(acts) == DOWN_GROUP + DOWN_LAG:
                acc = _swiglu_down(acts[:DOWN_GROUP], w, first, acc)
                del acts[:DOWN_GROUP]
                first += DOWN_GROUP
        while acts:
            acc = _swiglu_down(acts[:DOWN_GROUP], w, first, acc)
            del acts[:DOWN_GROUP]
            first += DOWN_GROUP
        o_ref[0] = h_prev_ref[...] + _rms_norm(acc, w["g_ffn_post"][...])

    @pl.when(k % 2 == 0)
    def _():
        step(h1_ref, f1_ref, h0_ref, f0_ref)

    @pl.when(k % 2 == 1)
    def _():
        step(h0_ref, f0_ref, h1_ref, f1_ref)


def _resident(shape):
    return pl.BlockSpec(shape, lambda *_: (0,) * len(shape),
                        pipeline_mode=pl.Buffered(1))


def _block_diag(w):
    groups, n, _ = w.shape
    eye = jnp.eye(groups, dtype=w.dtype)
    return (eye[:, None, :, None] * w[:, :, None, :]).reshape(groups * n, groups * n)


@functools.partial(jax.jit, static_argnames=("seq_tile",))
def _forward(x, meta_tokens, norm_mix_pre, w_in, conv_w, pool_w, pool_scale,
             w_out, norm_mix_post, norm_ffn_pre, w_gate, w_up, w_down,
             norm_ffn_post, seq_tile=SEQ_TILE):
    bsz, seq, d_model = x.shape
    depth, _, d_in_proj = w_in.shape
    d_conv = d_in_proj // 4
    d_ff = w_gate.shape[2]
    assert depth == 1, "single block only: meta-token outputs are never formed"
    assert meta_tokens.shape == (N_META, d_model)
    assert pool_w.shape[1] == len(POOL_WINDOWS) and pool_w.shape[2] == LANES
    assert pool_scale.shape[1] == d_conv == len(POOL_WINDOWS) * LANES
    assert conv_w.shape[1] - 1 <= CONV_HIST and max(POOL_WINDOWS) - 1 <= POOL_HIST
    assert seq % seq_tile == 0 and seq_tile % MIX_ROWS == 0 and seq_tile % POST_ROWS == 0
    assert d_ff % MXU_COLS == 0

    g_mix_pre = norm_mix_pre.astype(_F32)
    win = w_in[0].astype(_BF16)
    row = lambda v: v.reshape(1, -1).astype(_F32)

    meta_cv, meta_p = pl.pallas_call(
        _meta_kernel,
        out_shape=(jax.ShapeDtypeStruct((CONV_HIST, d_conv), _F32),
                   jax.ShapeDtypeStruct((POOL_HIST, d_conv), _F32)),
        name="meta_carry",
    )(meta_tokens.astype(_F32), g_mix_pre, win)

    consts = (
        meta_cv, meta_p, g_mix_pre, win, conv_w[0].astype(_F32),
        _block_diag(pool_w[0]).astype(_BF16), row(pool_scale[0]),
        w_out[0].astype(_BF16), row(norm_mix_post[0]), row(norm_ffn_pre[0]),
        w_gate[0].astype(_BF16), w_up[0].astype(_BF16), w_down[0].astype(_BF16),
        row(norm_ffn_post[0]),
    )
    assert len(consts) == len(_WEIGHT_NAMES)

    tiles_per_seq = seq // seq_tile
    n_tiles = bsz * tiles_per_seq

    def tile_index(t):
        return (t // tiles_per_seq, t % tiles_per_seq, 0)

    x_spec = pl.BlockSpec((1, seq_tile, d_model),
                          lambda k: tile_index(jnp.minimum(k, n_tiles - 1)))
    o_spec = pl.BlockSpec((1, seq_tile, d_model),
                          lambda k: tile_index(jnp.maximum(k - 1, 0)))
    groups = d_conv // LANES
    slot = [pltpu.VMEM((seq_tile, d_model), _F32), pltpu.VMEM((seq_tile, d_model), _BF16)]
    return pl.pallas_call(
        functools.partial(_block_kernel, tiles_per_seq),
        grid=(n_tiles + 1,),
        in_specs=[x_spec] + [_resident(c.shape) for c in consts],
        out_specs=o_spec,
        out_shape=jax.ShapeDtypeStruct(x.shape, x.dtype),
        scratch_shapes=[pltpu.VMEM((groups, CONV_HIST + seq_tile, LANES), _F32),
                        pltpu.VMEM((groups, POOL_HIST + seq_tile, LANES), _F32),
                        pltpu.VMEM((seq_tile, 2 * d_conv), _BF16),
                        pltpu.VMEM((seq_tile, d_conv), _BF16)] + slot + slot,
        compiler_params=pltpu.CompilerParams(
            dimension_semantics=("arbitrary",),
            vmem_limit_bytes=VMEM_LIMIT_BYTES),
        name="block",
    )(x, *consts)


def kernel(x, meta_tokens, norm_mix_pre, w_in, conv_w, pool_w, pool_scale, w_out,
           norm_mix_post, norm_ffn_pre, w_gate, w_up, w_down, norm_ffn_post):
    return _forward(x, meta_tokens, norm_mix_pre, w_in, conv_w, pool_w, pool_scale,
                    w_out, norm_mix_post, norm_ffn_pre, w_gate, w_up, w_down,
                    norm_ffn_post)
```

```python
import functools

import jax
import jax.numpy as jnp
from jax import lax
from jax.experimental import pallas as pl
from jax.experimental.pallas import tpu as pltpu

RMS_EPS = 1e-6
POOL_WINDOWS = (2, 4, 8, 16)
N_META = 16
SUBLANES = 8
LANES = 128
MXU_COLS = 256
CONV_HIST = SUBLANES
POOL_HIST = 16
SEQ_TILE = 512
POST_ROWS = 16
MIX_IN_LO_AT, MIX_IN_HI_AT, MIX_POOL_AT, MIX_OUT_AT, MIX_POST_AT = 2, 3, 5, 7, 9
DOWN_GROUP = 3
DOWN_LAG = 2
VMEM_LIMIT_BYTES = 56 * 1024 * 1024

_BF16 = jnp.bfloat16
_F32 = jnp.float32


def _rms_norm(x, g):
    y = x * lax.rsqrt(jnp.mean(x * x, axis=-1, keepdims=True) + RMS_EPS)
    return y * g


def _dot(a, b):
    return jnp.dot(a, b, preferred_element_type=_F32)


def _project_in_half(a, win_ref, half):
    n = win_ref.shape[1] // 2
    return _dot(a, win_ref[:, half * n:(half + 1) * n])


def _split_projection(z_lo, z_hi):
    d_conv = z_lo.shape[1] // 2
    return z_lo[:, :d_conv], z_lo[:, d_conv:] * z_hi[:, :d_conv], z_hi[:, d_conv:]


def _project_in(x, g_ref, win_ref):
    a = _rms_norm(x, g_ref[...]).astype(_BF16)
    return _split_projection(_project_in_half(a, win_ref, 0), _project_in_half(a, win_ref, 1))


def _prep_kernel(meta_ref, g_ref, win_ref, poolw_ref, pscale_ref, wout_pool_ref,
                 cv_ref, p_ref, wfold_ref):
    _, cv, p = _project_in(meta_ref[...], g_ref, win_ref)
    cv_ref[...] = cv[N_META - CONV_HIST:]
    p_ref[...] = p[N_META - POOL_HIST:]
    for g in range(poolw_ref.shape[0]):
        rows = slice(g * LANES, (g + 1) * LANES)
        scaled = poolw_ref[g] * pscale_ref[:, rows]
        wfold_ref[rows, :] = jnp.dot(scaled, wout_pool_ref[rows, :],
                                     preferred_element_type=_F32,
                                     precision=lax.Precision.HIGHEST)


def _causal_conv(cv_hist, cv, w):
    u = jnp.concatenate([cv_hist, cv], axis=0)
    k = w.shape[0]
    y = w[0:1] * pltpu.roll(u, k - 1, 0)
    for i in range(1, k):
        shifted = pltpu.roll(u, k - 1 - i, 0) if i < k - 1 else u
        y = y + w[i:i + 1] * shifted
    return y[CONV_HIST:]


def _trailing_mean_minus_token(p_hist, p):
    group = p.shape[1] // len(POOL_WINDOWS)
    s = jnp.concatenate([p_hist, p], axis=0)
    means = []
    win = 1
    for target in POOL_WINDOWS:
        while win < target:
            s = s + pltpu.roll(s, win, 0)
            win *= 2
        means.append(s[:, 0:group] * (1.0 / target))
        s = s[:, group:]
    return jnp.concatenate(means, axis=1)[POOL_HIST:] - p


def _mix_conv_pool(b_gate, cv, p, w, cv_hist_ref, p_hist_ref):
    tile = cv.shape[0]
    y_conv = b_gate * _causal_conv(cv_hist_ref[...], cv, w["convw"][...])
    pooled = _trailing_mean_minus_token(p_hist_ref[...], p)
    cv_hist_ref[...] = cv[tile - CONV_HIST:]
    p_hist_ref[...] = p[tile - POOL_HIST:]
    return y_conv.astype(_BF16), pooled.astype(_BF16)


def _zero_after(v):
    bits = lax.bitcast_convert_type(v, jnp.uint32)
    return lax.bitcast_convert_type((bits >> 16) >> 16, _F32)


def _mix_post(x_ref, m_ref, w, h_ref, f_ref):
    g_post, g_pre = w["g_mix_post"][...], w["g_ffn_pre"][...]
    anchor = None
    for r in range(0, m_ref.shape[0], POST_ROWS):
        rows = slice(r, r + POST_ROWS)
        m = m_ref[rows, :]
        ms = jnp.mean(m * m, axis=-1, keepdims=True)
        if anchor is not None:
            ms = ms + anchor
        h = x_ref[0, rows, :] + m * lax.rsqrt(ms + RMS_EPS) * g_post
        rs = lax.rsqrt(jnp.mean(h * h, axis=-1, keepdims=True) + RMS_EPS)
        anchor = _zero_after(rs)
        h_ref[rows, :] = h
        f_ref[rows, :] = (h * rs * g_pre).astype(_BF16)


def _swiglu_act(f, w, j):
    cols = slice(j * MXU_COLS, (j + 1) * MXU_COLS)
    gate = _dot(f, w["wg"][:, cols])
    up = _dot(f, w["wu"][:, cols])
    return (gate / (1.0 + jnp.exp(-gate)) * up).astype(_BF16)


def _swiglu_down(acts, w, j, acc):
    rows = slice(j * MXU_COLS, (j + len(acts)) * MXU_COLS)
    part = _dot(jnp.concatenate(acts, axis=1), w["wd"][rows, :])
    return part if acc is None else acc + part


_WEIGHT_NAMES = ("mcv", "mp", "g_mix_pre", "win", "convw",
                 "wout", "g_mix_post", "g_ffn_pre", "wg", "wu", "wd", "g_ffn_post")


def _block_kernel(tiles_per_seq, x_ref, *refs):
    w = dict(zip(_WEIGHT_NAMES, refs))
    o_ref, cv_hist_ref, p_hist_ref, h0_ref, f0_ref, h1_ref, f1_ref = refs[len(_WEIGHT_NAMES):]
    k = pl.program_id(0)

    @pl.when(k == 0)
    def _():
        h1_ref[...] = jnp.zeros_like(h1_ref)
        f1_ref[...] = jnp.zeros_like(f1_ref)

    @pl.when(k % tiles_per_seq == 0)
    def _():
        cv_hist_ref[...] = w["mcv"][...]
        p_hist_ref[...] = w["mp"][...]

    n_chunks = w["wd"].shape[0] // MXU_COLS

    def step(h_prev_ref, f_prev_ref, h_next_ref, f_next_ref):
        x = x_ref[0]
        f_prev = f_prev_ref[...]
        mix = {}

        def mix_in_lo():
            mix["a"] = _rms_norm(x, w["g_mix_pre"][...]).astype(_BF16)
            mix["z_lo"] = _project_in_half(mix["a"], w["win"], 0)

        def mix_in_hi():
            mix["z_hi"] = _project_in_half(mix.pop("a"), w["win"], 1)

        def mix_conv_pool():
            b_gate, cv, p = _split_projection(mix.pop("z_lo"), mix.pop("z_hi"))
            mix["y_conv"], mix["pooled"] = _mix_conv_pool(b_gate, cv, p, w, cv_hist_ref, p_hist_ref)

        def mix_out_proj():
            mix_in = jnp.concatenate([mix.pop("y_conv"), mix.pop("pooled")], axis=1)
            h_next_ref[...] = _dot(mix_in, w["wout"][...])

        def mix_post():
            _mix_post(x_ref, h_next_ref, w, h_next_ref, f_next_ref)

        before_chunk = {MIX_IN_LO_AT: mix_in_lo, MIX_IN_HI_AT: mix_in_hi,
                        MIX_POOL_AT: mix_conv_pool, MIX_OUT_AT: mix_out_proj,
                        MIX_POST_AT: mix_post}
        acc = None
        acts = []
        first = 0
        for j in range(n_chunks):
            if j in before_chunk:
                before_chunk[j]()
            acts.append(_swiglu_act(f_prev, w, j))
            if len(acts) == DOWN_GROUP + DOWN_LAG:
                acc = _swiglu_down(acts[:DOWN_GROUP], w, first, acc)
                del acts[:DOWN_GROUP]
                first += DOWN_GROUP
        while acts:
            acc = _swiglu_down(acts[:DOWN_GROUP], w, first, acc)
            del acts[:DOWN_GROUP]
            first += DOWN_GROUP
        o_ref[0] = h_prev_ref[...] + _rms_norm(acc, w["g_ffn_post"][...])

    @pl.when(k % 2 == 0)
    def _():
        step(h1_ref, f1_ref, h0_ref, f0_ref)

    @pl.when(k % 2 == 1)
    def _():
        step(h0_ref, f0_ref, h1_ref, f1_ref)


def _resident(shape):
    return pl.BlockSpec(shape, lambda *_: (0,) * len(shape),
                        pipeline_mode=pl.Buffered(1))


@functools.partial(jax.jit, static_argnames=("seq_tile",))
def _forward(x, meta_tokens, norm_mix_pre, w_in, conv_w, pool_w, pool_scale,
             w_out, norm_mix_post, norm_ffn_pre, w_gate, w_up, w_down,
             norm_ffn_post, seq_tile=SEQ_TILE):
    bsz, seq, d_model = x.shape
    depth, _, d_in_proj = w_in.shape
    d_conv = d_in_proj // 4
    d_ff = w_gate.shape[2]
    assert depth == 1, "single block only: meta-token outputs are never formed"
    assert meta_tokens.shape == (N_META, d_model)
    assert pool_w.shape[1] == len(POOL_WINDOWS) and pool_w.shape[2] == LANES
    assert pool_scale.shape[1] == d_conv and w_out.shape[1] == 2 * d_conv
    assert seq % seq_tile == 0 and seq_tile % POOL_HIST == 0
    assert d_ff % MXU_COLS == 0

    g_mix_pre = norm_mix_pre.astype(_F32)
    win = w_in[0].astype(_BF16)
    row = lambda v: v.reshape(1, -1).astype(_F32)

    meta_cv, meta_p, wout_pool = pl.pallas_call(
        _prep_kernel,
        out_shape=(jax.ShapeDtypeStruct((CONV_HIST, d_conv), _F32),
                   jax.ShapeDtypeStruct((POOL_HIST, d_conv), _F32),
                   jax.ShapeDtypeStruct((d_conv, d_model), _F32)),
        name="prep",
    )(meta_tokens.astype(_F32), g_mix_pre, win, pool_w[0].astype(_F32),
      row(pool_scale[0]), w_out[0, d_conv:].astype(_F32))
    wout = jnp.concatenate([w_out[0, :d_conv], wout_pool], axis=0).astype(_BF16)

    consts = (
        meta_cv, meta_p, g_mix_pre, win, conv_w[0].astype(_F32),
        wout, row(norm_mix_post[0]), row(norm_ffn_pre[0]),
        w_gate[0].astype(_BF16), w_up[0].astype(_BF16), w_down[0].astype(_BF16),
        row(norm_ffn_post[0]),
    )
    assert len(consts) == len(_WEIGHT_NAMES)

    tiles_per_seq = seq // seq_tile
    n_tiles = bsz * tiles_per_seq

    def tile_index(t):
        return (t // tiles_per_seq, t % tiles_per_seq, 0)

    x_spec = pl.BlockSpec((1, seq_tile, d_model),
                          lambda k: tile_index(jnp.minimum(k, n_tiles - 1)))
    o_spec = pl.BlockSpec((1, seq_tile, d_model),
                          lambda k: tile_index(jnp.maximum(k - 1, 0)))
    slot = [pltpu.VMEM((seq_tile, d_model), _F32), pltpu.VMEM((seq_tile, d_model), _BF16)]
    return pl.pallas_call(
        functools.partial(_block_kernel, tiles_per_seq),
        grid=(n_tiles + 1,),
        in_specs=[x_spec] + [_resident(c.shape) for c in consts],
        out_specs=o_spec,
        out_shape=jax.ShapeDtypeStruct(x.shape, x.dtype),
        scratch_shapes=[pltpu.VMEM((CONV_HIST, d_conv), _F32),
                        pltpu.VMEM((POOL_HIST, d_conv), _F32)] + slot + slot,
        compiler_params=pltpu.CompilerParams(
            dimension_semantics=("arbitrary",),
            vmem_limit_bytes=VMEM_LIMIT_BYTES),
        name="block",
    )(x, *consts)


def kernel(x, meta_tokens, norm_mix_pre, w_in, conv_w, pool_w, pool_scale, w_out,
           norm_mix_post, norm_ffn_pre, w_gate, w_up, w_down, norm_ffn_post):
    return _forward(x, meta_tokens, norm_mix_pre, w_in, conv_w, pool_w, pool_scale,
                    w_out, norm_mix_post, norm_ffn_pre, w_gate, w_up, w_down,
                    norm_ffn_post)
```

```python
import functools

import jax
import jax.numpy as jnp
from jax import lax
from jax.experimental import pallas as pl
from jax.experimental.pallas import tpu as pltpu

RMS_EPS = 1e-6
POOL_WINDOWS = (2, 4, 8, 16)
N_META = 16
SUBLANES = 8
LANES = 128
MXU_COLS = 256
CONV_HIST = SUBLANES
POOL_HIST = 16
SEQ_TILE = 512
POST_ROWS = 16
MIX_IN_LO_AT, MIX_IN_HI_AT, MIX_POOL_AT, MIX_OUT_AT, MIX_POST_AT = 2, 3, 5, 7, 9
DOWN_GROUP = 3
DOWN_LAG = 2
VMEM_LIMIT_BYTES = 56 * 1024 * 1024

_BF16 = jnp.bfloat16
_F32 = jnp.float32


def _rms_norm(x, g):
    y = x * lax.rsqrt(jnp.mean(x * x, axis=-1, keepdims=True) + RMS_EPS)
    return y * g


def _dot(a, b):
    return jnp.dot(a, b, preferred_element_type=_F32)


def _project_in_half(a, win_ref, half):
    n = win_ref.shape[1] // 2
    return _dot(a, win_ref[:, half * n:(half + 1) * n])


def _split_projection(z_lo, z_hi):
    d_conv = z_lo.shape[1] // 2
    return z_lo[:, :d_conv], z_lo[:, d_conv:] * z_hi[:, :d_conv], z_hi[:, d_conv:]


def _project_in(x, g_ref, win_ref):
    a = _rms_norm(x, g_ref[...]).astype(_BF16)
    return _split_projection(_project_in_half(a, win_ref, 0), _project_in_half(a, win_ref, 1))


def _prep_kernel(meta_ref, g_ref, win_ref, poolw_ref, pscale_ref, wout_ref,
                 cv_ref, p_ref, wout_fold_ref):
    _, cv, p = _project_in(meta_ref[...], g_ref, win_ref)
    cv_ref[...] = cv[N_META - CONV_HIST:]
    p_ref[...] = p[N_META - POOL_HIST:]
    d_conv = pscale_ref.shape[1]
    wout_fold_ref[0:d_conv, :] = wout_ref[0:d_conv, :].astype(_BF16)
    for g in range(poolw_ref.shape[0]):
        cols = slice(g * LANES, (g + 1) * LANES)
        rows = slice(d_conv + g * LANES, d_conv + (g + 1) * LANES)
        scaled = poolw_ref[g] * pscale_ref[:, cols]
        wout_fold_ref[rows, :] = jnp.dot(scaled, wout_ref[rows, :],
                                         preferred_element_type=_F32,
                                         precision=lax.Precision.HIGHEST).astype(_BF16)


def _causal_conv(cv_hist, cv, w):
    u = jnp.concatenate([cv_hist, cv], axis=0)
    k = w.shape[0]
    y = w[0:1] * pltpu.roll(u, k - 1, 0)
    for i in range(1, k):
        shifted = pltpu.roll(u, k - 1 - i, 0) if i < k - 1 else u
        y = y + w[i:i + 1] * shifted
    return y[CONV_HIST:]


def _trailing_mean_minus_token(p_hist, p):
    group = p.shape[1] // len(POOL_WINDOWS)
    s = jnp.concatenate([p_hist, p], axis=0)
    means = []
    win = 1
    for target in POOL_WINDOWS:
        while win < target:
            s = s + pltpu.roll(s, win, 0)
            win *= 2
        means.append(s[:, 0:group] * (1.0 / target))
        s = s[:, group:]
    return jnp.concatenate(means, axis=1)[POOL_HIST:] - p


def _mix_conv_pool(b_gate, cv, p, w, cv_hist_ref, p_hist_ref):
    tile = cv.shape[0]
    y_conv = b_gate * _causal_conv(cv_hist_ref[...], cv, w["convw"][...])
    pooled = _trailing_mean_minus_token(p_hist_ref[...], p)
    cv_hist_ref[...] = cv[tile - CONV_HIST:]
    p_hist_ref[...] = p[tile - POOL_HIST:]
    return y_conv.astype(_BF16), pooled.astype(_BF16)


def _zero_after(v):
    bits = lax.bitcast_convert_type(v, jnp.uint32)
    return lax.bitcast_convert_type((bits >> 16) >> 16, _F32)


def _mix_post(x_ref, m_ref, w, h_ref, f_ref):
    g_post, g_pre = w["g_mix_post"][...], w["g_ffn_pre"][...]
    anchor = None
    for r in range(0, m_ref.shape[0], POST_ROWS):
        rows = slice(r, r + POST_ROWS)
        m = m_ref[rows, :]
        ms = jnp.mean(m * m, axis=-1, keepdims=True)
        if anchor is not None:
            ms = ms + anchor
        h = x_ref[0, rows, :] + m * lax.rsqrt(ms + RMS_EPS) * g_post
        rs = lax.rsqrt(jnp.mean(h * h, axis=-1, keepdims=True) + RMS_EPS)
        anchor = _zero_after(rs)
        h_ref[rows, :] = h
        f_ref[rows, :] = (h * rs * g_pre).astype(_BF16)


def _swiglu_act(f, w, j):
    cols = slice(j * MXU_COLS, (j + 1) * MXU_COLS)
    gate = _dot(f, w["wg"][:, cols])
    up = _dot(f, w["wu"][:, cols])
    return (gate / (1.0 + jnp.exp(-gate)) * up).astype(_BF16)


def _swiglu_down(acts, w, j, acc):
    rows = slice(j * MXU_COLS, (j + len(acts)) * MXU_COLS)
    part = _dot(jnp.concatenate(acts, axis=1), w["wd"][rows, :])
    return part if acc is None else acc + part


_WEIGHT_NAMES = ("mcv", "mp", "g_mix_pre", "win", "convw",
                 "wout", "g_mix_post", "g_ffn_pre", "wg", "wu", "wd", "g_ffn_post")


def _block_kernel(tiles_per_seq, x_ref, *refs):
    w = dict(zip(_WEIGHT_NAMES, refs))
    o_ref, cv_hist_ref, p_hist_ref, h_slots_ref, f_slots_ref = refs[len(_WEIGHT_NAMES):]
    k = pl.program_id(0)
    last = pl.num_programs(0) - 1

    @pl.when(k % tiles_per_seq == 0)
    def _():
        cv_hist_ref[...] = w["mcv"][...]
        p_hist_ref[...] = w["mp"][...]

    n_chunks = w["wd"].shape[0] // MXU_COLS

    def step(h_prev_ref, f_prev_ref, h_next_ref, f_next_ref, mixer=True, swiglu=True):
        x = x_ref[0]
        f_prev = f_prev_ref[...] if swiglu else None
        mix = {}

        def mix_in_lo():
            mix["a"] = _rms_norm(x, w["g_mix_pre"][...]).astype(_BF16)
            mix["z_lo"] = _project_in_half(mix["a"], w["win"], 0)

        def mix_in_hi():
            mix["z_hi"] = _project_in_half(mix.pop("a"), w["win"], 1)

        def mix_conv_pool():
            b_gate, cv, p = _split_projection(mix.pop("z_lo"), mix.pop("z_hi"))
            mix["y_conv"], mix["pooled"] = _mix_conv_pool(b_gate, cv, p, w, cv_hist_ref, p_hist_ref)

        def mix_out_proj():
            mix_in = jnp.concatenate([mix.pop("y_conv"), mix.pop("pooled")], axis=1)
            h_next_ref[...] = _dot(mix_in, w["wout"][...])

        def mix_post():
            _mix_post(x_ref, h_next_ref, w, h_next_ref, f_next_ref)

        before_chunk = {MIX_IN_LO_AT: mix_in_lo, MIX_IN_HI_AT: mix_in_hi,
                        MIX_POOL_AT: mix_conv_pool, MIX_OUT_AT: mix_out_proj,
                        MIX_POST_AT: mix_post}
        acc = None
        acts = []
        first = 0
        for j in range(n_chunks):
            if mixer and j in before_chunk:
                before_chunk[j]()
            if not swiglu:
                continue
            acts.append(_swiglu_act(f_prev, w, j))
            if len(acts) == DOWN_GROUP + DOWN_LAG:
                acc = _swiglu_down(acts[:DOWN_GROUP], w, first, acc)
                del acts[:DOWN_GROUP]
                first += DOWN_GROUP
        while acts:
            acc = _swiglu_down(acts[:DOWN_GROUP], w, first, acc)
            del acts[:DOWN_GROUP]
            first += DOWN_GROUP
        if swiglu:
            o_ref[0] = h_prev_ref[...] + _rms_norm(acc, w["g_ffn_post"][...])

    nxt = k % 2
    prev = 1 - nxt
    slots = (h_slots_ref.at[prev], f_slots_ref.at[prev], h_slots_ref.at[nxt], f_slots_ref.at[nxt])

    @pl.when(k == 0)
    def _():
        step(*slots, swiglu=False)

    @pl.when((k > 0) & (k < last))
    def _():
        step(*slots)

    @pl.when(k == last)
    def _():
        step(*slots, mixer=False)


def _resident(shape):
    return pl.BlockSpec(shape, lambda *_: (0,) * len(shape),
                        pipeline_mode=pl.Buffered(1))


@functools.partial(jax.jit, static_argnames=("seq_tile",))
def _forward(x, meta_tokens, norm_mix_pre, w_in, conv_w, pool_w, pool_scale,
             w_out, norm_mix_post, norm_ffn_pre, w_gate, w_up, w_down,
             norm_ffn_post, seq_tile=SEQ_TILE):
    bsz, seq, d_model = x.shape
    depth, _, d_in_proj = w_in.shape
    d_conv = d_in_proj // 4
    d_ff = w_gate.shape[2]
    assert depth == 1, "single block only: meta-token outputs are never formed"
    assert meta_tokens.shape == (N_META, d_model)
    assert pool_w.shape[1] == len(POOL_WINDOWS) and pool_w.shape[2] == LANES
    assert pool_scale.shape[1] == d_conv and w_out.shape[1] == 2 * d_conv
    assert seq % seq_tile == 0 and seq_tile % POOL_HIST == 0
    assert d_ff % MXU_COLS == 0

    g_mix_pre = norm_mix_pre.astype(_F32)
    win = w_in[0].astype(_BF16)
    row = lambda v: v.reshape(1, -1).astype(_F32)

    meta_cv, meta_p, wout = pl.pallas_call(
        _prep_kernel,
        out_shape=(jax.ShapeDtypeStruct((CONV_HIST, d_conv), _F32),
                   jax.ShapeDtypeStruct((POOL_HIST, d_conv), _F32),
                   jax.ShapeDtypeStruct((2 * d_conv, d_model), _BF16)),
        name="prep",
    )(meta_tokens.astype(_F32), g_mix_pre, win, pool_w[0].astype(_F32),
      row(pool_scale[0]), w_out[0].astype(_F32))

    consts = (
        meta_cv, meta_p, g_mix_pre, win, conv_w[0].astype(_F32),
        wout, row(norm_mix_post[0]), row(norm_ffn_pre[0]),
        w_gate[0].astype(_BF16), w_up[0].astype(_BF16), w_down[0].astype(_BF16),
        row(norm_ffn_post[0]),
    )
    assert len(consts) == len(_WEIGHT_NAMES)

    tiles_per_seq = seq // seq_tile
    n_tiles = bsz * tiles_per_seq

    def tile_index(t):
        return (t // tiles_per_seq, t % tiles_per_seq, 0)

    x_spec = pl.BlockSpec((1, seq_tile, d_model),
                          lambda k: tile_index(jnp.minimum(k, n_tiles - 1)))
    o_spec = pl.BlockSpec((1, seq_tile, d_model),
                          lambda k: tile_index(jnp.maximum(k - 1, 0)))
    slots = [pltpu.VMEM((2, seq_tile, d_model), _F32), pltpu.VMEM((2, seq_tile, d_model), _BF16)]
    return pl.pallas_call(
        functools.partial(_block_kernel, tiles_per_seq),
        grid=(n_tiles + 1,),
        in_specs=[x_spec] + [_resident(c.shape) for c in consts],
        out_specs=o_spec,
        out_shape=jax.ShapeDtypeStruct(x.shape, x.dtype),
        scratch_shapes=[pltpu.VMEM((CONV_HIST, d_conv), _F32),
                        pltpu.VMEM((POOL_HIST, d_conv), _F32)] + slots,
        compiler_params=pltpu.CompilerParams(
            dimension_semantics=("arbitrary",),
            vmem_limit_bytes=VMEM_LIMIT_BYTES),
        name="block",
    )(x, *consts)


def kernel(x, meta_tokens, norm_mix_pre, w_in, conv_w, pool_w, pool_scale, w_out,
           norm_mix_post, norm_ffn_pre, w_gate, w_up, w_down, norm_ffn_post):
    return _forward(x, meta_tokens, norm_mix_pre, w_in, conv_w, pool_w, pool_scale,
                    w_out, norm_mix_post, norm_ffn_pre, w_gate, w_up, w_down,
                    norm_ffn_post)
```

```python
import functools

import jax
import jax.numpy as jnp
from jax import lax
from jax.experimental import pallas as pl
from jax.experimental.pallas import tpu as pltpu

RMS_EPS = 1e-6
POOL_WINDOWS = (2, 4, 8, 16)
N_META = 16
SUBLANES = 8
LANES = 128
MXU_COLS = 256
CONV_HIST = SUBLANES
POOL_HIST = 16
SEQ_TILE = 512
POST_ROWS = 16
MIX_IN_LO_AT, MIX_IN_HI_AT, MIX_POOL_AT, MIX_OUT_AT, MIX_POST_AT = 2, 3, 5, 7, 9
DOWN_GROUP = 3
DOWN_LAG = 2
VMEM_LIMIT_BYTES = 56 * 1024 * 1024

_BF16 = jnp.bfloat16
_F32 = jnp.float32


def _rms_norm(x, g):
    y = x * lax.rsqrt(jnp.mean(x * x, axis=-1, keepdims=True) + RMS_EPS)
    return y * g


def _dot(a, b):
    return jnp.dot(a, b, preferred_element_type=_F32)


def _project_in_half(a, win_ref, half):
    n = win_ref.shape[1] // 2
    return _dot(a, win_ref[:, half * n:(half + 1) * n])


def _split_projection(z_lo, z_hi):
    d_conv = z_lo.shape[1] // 2
    return z_lo[:, :d_conv], z_lo[:, d_conv:] * z_hi[:, :d_conv], z_hi[:, d_conv:]


def _project_in(x, g_ref, win_ref):
    a = _rms_norm(x, g_ref[...]).astype(_BF16)
    return _split_projection(_project_in_half(a, win_ref, 0), _project_in_half(a, win_ref, 1))


def _prep_kernel(meta_ref, g_ref, win_ref, poolw_ref, pscale_ref, wout_ref,
                 cv_ref, p_ref, wout_fold_ref):
    _, cv, p = _project_in(meta_ref[...], g_ref, win_ref)
    cv_ref[...] = cv[N_META - CONV_HIST:]
    p_ref[...] = p[N_META - POOL_HIST:]
    d_conv = pscale_ref.shape[1]
    wout_fold_ref[0:d_conv, :] = wout_ref[0:d_conv, :].astype(_BF16)
    for g in range(poolw_ref.shape[0]):
        cols = slice(g * LANES, (g + 1) * LANES)
        rows = slice(d_conv + g * LANES, d_conv + (g + 1) * LANES)
        scaled = poolw_ref[g] * pscale_ref[:, cols]
        wout_fold_ref[rows, :] = jnp.dot(scaled, wout_ref[rows, :],
                                         preferred_element_type=_F32,
                                         precision=lax.Precision.HIGHEST).astype(_BF16)


def _causal_conv(cv_hist, cv, w):
    u = jnp.concatenate([cv_hist, cv], axis=0)
    k = w.shape[0]
    y = w[0:1] * pltpu.roll(u, k - 1, 0)
    for i in range(1, k):
        shifted = pltpu.roll(u, k - 1 - i, 0) if i < k - 1 else u
        y = y + w[i:i + 1] * shifted
    return y[CONV_HIST:]


def _trailing_mean_minus_token(p_hist, p):
    group = p.shape[1] // len(POOL_WINDOWS)
    s = jnp.concatenate([p_hist, p], axis=0)
    means = []
    win = 1
    for target in POOL_WINDOWS:
        while win < target:
            s = s + pltpu.roll(s, win, 0)
            win *= 2
        means.append(s[:, 0:group] * (1.0 / target))
        s = s[:, group:]
    return jnp.concatenate(means, axis=1)[POOL_HIST:] - p


def _mix_conv_pool(b_gate, cv, p, w, cv_hist_ref, p_hist_ref):
    tile = cv.shape[0]
    y_conv = b_gate * _causal_conv(cv_hist_ref[...], cv, w["convw"][...])
    pooled = _trailing_mean_minus_token(p_hist_ref[...], p)
    cv_hist_ref[...] = cv[tile - CONV_HIST:]
    p_hist_ref[...] = p[tile - POOL_HIST:]
    return y_conv.astype(_BF16), pooled.astype(_BF16)


def _zero_after(v):
    bits = lax.bitcast_convert_type(v, jnp.uint32)
    return lax.bitcast_convert_type((bits >> 16) >> 16, _F32)


def _mix_post(x_ref, m_ref, w, h_ref, f_ref):
    g_post, g_pre = w["g_mix_post"][...], w["g_ffn_pre"][...]
    anchor = None
    for r in range(0, m_ref.shape[0], POST_ROWS):
        rows = slice(r, r + POST_ROWS)
        m = m_ref[rows, :]
        ms = jnp.mean(m * m, axis=-1, keepdims=True)
        if anchor is not None:
            ms = ms + anchor
        h = x_ref[0, rows, :] + m * lax.rsqrt(ms + RMS_EPS) * g_post
        rs = lax.rsqrt(jnp.mean(h * h, axis=-1, keepdims=True) + RMS_EPS)
        anchor = _zero_after(rs)
        h_ref[rows, :] = h
        f_ref[rows, :] = (h * rs * g_pre).astype(_BF16)


def _swiglu_act(f, w, j):
    cols = slice(j * MXU_COLS, (j + 1) * MXU_COLS)
    gate = _dot(f, w["wg"][:, cols])
    up = _dot(f, w["wu"][:, cols])
    return (gate / (1.0 + jnp.exp(-gate)) * up).astype(_BF16)


def _swiglu_down(acts, w, j, acc):
    rows = slice(j * MXU_COLS, (j + len(acts)) * MXU_COLS)
    part = _dot(jnp.concatenate(acts, axis=1), w["wd"][rows, :])
    return part if acc is None else acc + part


_WEIGHT_NAMES = ("mcv", "mp", "g_mix_pre", "win", "convw",
                 "wout", "g_mix_post", "g_ffn_pre", "wg", "wu", "wd", "g_ffn_post")


def _block_kernel(tiles_per_seq, x_ref, *refs):
    w = dict(zip(_WEIGHT_NAMES, refs))
    o_ref, cv_hist_ref, p_hist_ref, z_ref, h_slots_ref, f_slots_ref = refs[len(_WEIGHT_NAMES):]
    k = pl.program_id(0)
    last = pl.num_programs(0) - 1

    @pl.when(k % tiles_per_seq == 0)
    def _():
        cv_hist_ref[...] = w["mcv"][...]
        p_hist_ref[...] = w["mp"][...]

    n_chunks = w["wd"].shape[0] // MXU_COLS

    def step(h_prev_ref, f_prev_ref, h_next_ref, f_next_ref, mixer=True, swiglu=True):
        x = x_ref[0]
        f_prev = f_prev_ref[...] if swiglu else None
        mix = {}

        def mix_in_lo():
            mix["a"] = _rms_norm(x, w["g_mix_pre"][...]).astype(_BF16)
            half = z_ref.shape[1] // 2
            z_ref[:, :half] = _project_in_half(mix["a"], w["win"], 0)

        def mix_in_hi():
            half = z_ref.shape[1] // 2
            z_ref[:, half:] = _project_in_half(mix.pop("a"), w["win"], 1)

        def mix_conv_pool():
            half = z_ref.shape[1] // 2
            b_gate, cv, p = _split_projection(z_ref[:, :half], z_ref[:, half:])
            mix["y_conv"], mix["pooled"] = _mix_conv_pool(b_gate, cv, p, w, cv_hist_ref, p_hist_ref)

        def mix_out_proj():
            mix_in = jnp.concatenate([mix.pop("y_conv"), mix.pop("pooled")], axis=1)
            h_next_ref[...] = _dot(mix_in, w["wout"][...])

        def mix_post():
            _mix_post(x_ref, h_next_ref, w, h_next_ref, f_next_ref)

        before_chunk = {MIX_IN_LO_AT: mix_in_lo, MIX_IN_HI_AT: mix_in_hi,
                        MIX_POOL_AT: mix_conv_pool, MIX_OUT_AT: mix_out_proj,
                        MIX_POST_AT: mix_post}
        acc = None
        acts = []
        first = 0
        for j in range(n_chunks):
            if mixer and j in before_chunk:
                before_chunk[j]()
            if not swiglu:
                continue
            acts.append(_swiglu_act(f_prev, w, j))
            if len(acts) == DOWN_GROUP + DOWN_LAG:
                acc = _swiglu_down(acts[:DOWN_GROUP], w, first, acc)
                del acts[:DOWN_GROUP]
                first += DOWN_GROUP
        while acts:
            acc = _swiglu_down(acts[:DOWN_GROUP], w, first, acc)
            del acts[:DOWN_GROUP]
            first += DOWN_GROUP
        if swiglu:
            o_ref[0] = h_prev_ref[...] + _rms_norm(acc, w["g_ffn_post"][...])

    nxt = k % 2
    prev = 1 - nxt
    slots = (h_slots_ref.at[prev], f_slots_ref.at[prev], h_slots_ref.at[nxt], f_slots_ref.at[nxt])

    @pl.when(k == 0)
    def _():
        step(*slots, swiglu=False)

    @pl.when((k > 0) & (k < last))
    def _():
        step(*slots)

    @pl.when(k == last)
    def _():
        step(*slots, mixer=False)


def _resident(shape):
    return pl.BlockSpec(shape, lambda *_: (0,) * len(shape),
                        pipeline_mode=pl.Buffered(1))


@functools.partial(jax.jit, static_argnames=("seq_tile",))
def _forward(x, meta_tokens, norm_mix_pre, w_in, conv_w, pool_w, pool_scale,
             w_out, norm_mix_post, norm_ffn_pre, w_gate, w_up, w_down,
             norm_ffn_post, seq_tile=SEQ_TILE):
    bsz, seq, d_model = x.shape
    depth, _, d_in_proj = w_in.shape
    d_conv = d_in_proj // 4
    d_ff = w_gate.shape[2]
    assert depth == 1, "single block only: meta-token outputs are never formed"
    assert meta_tokens.shape == (N_META, d_model)
    assert pool_w.shape[1] == len(POOL_WINDOWS) and pool_w.shape[2] == LANES
    assert pool_scale.shape[1] == d_conv and w_out.shape[1] == 2 * d_conv
    assert seq % seq_tile == 0 and seq_tile % POOL_HIST == 0
    assert d_ff % MXU_COLS == 0

    g_mix_pre = norm_mix_pre.astype(_F32)
    win = w_in[0].astype(_BF16)
    row = lambda v: v.reshape(1, -1).astype(_F32)

    meta_cv, meta_p, wout = pl.pallas_call(
        _prep_kernel,
        out_shape=(jax.ShapeDtypeStruct((CONV_HIST, d_conv), _F32),
                   jax.ShapeDtypeStruct((POOL_HIST, d_conv), _F32),
                   jax.ShapeDtypeStruct((2 * d_conv, d_model), _BF16)),
        name="prep",
    )(meta_tokens.astype(_F32), g_mix_pre, win, pool_w[0].astype(_F32),
      row(pool_scale[0]), w_out[0].astype(_F32))

    consts = (
        meta_cv, meta_p, g_mix_pre, win, conv_w[0].astype(_F32),
        wout, row(norm_mix_post[0]), row(norm_ffn_pre[0]),
        w_gate[0].astype(_BF16), w_up[0].astype(_BF16), w_down[0].astype(_BF16),
        row(norm_ffn_post[0]),
    )
    assert len(consts) == len(_WEIGHT_NAMES)

    tiles_per_seq = seq // seq_tile
    n_tiles = bsz * tiles_per_seq

    def tile_index(t):
        return (t // tiles_per_seq, t % tiles_per_seq, 0)

    x_spec = pl.BlockSpec((1, seq_tile, d_model),
                          lambda k: tile_index(jnp.minimum(k, n_tiles - 1)))
    o_spec = pl.BlockSpec((1, seq_tile, d_model),
                          lambda k: tile_index(jnp.maximum(k - 1, 0)))
    slots = [pltpu.VMEM((2, seq_tile, d_model), _F32), pltpu.VMEM((2, seq_tile, d_model), _BF16)]
    return pl.pallas_call(
        functools.partial(_block_kernel, tiles_per_seq),
        grid=(n_tiles + 1,),
        in_specs=[x_spec] + [_resident(c.shape) for c in consts],
        out_specs=o_spec,
        out_shape=jax.ShapeDtypeStruct(x.shape, x.dtype),
        scratch_shapes=[pltpu.VMEM((CONV_HIST, d_conv), _F32),
                        pltpu.VMEM((POOL_HIST, d_conv), _F32),
                        pltpu.VMEM((seq_tile, d_in_proj), _F32)] + slots,
        compiler_params=pltpu.CompilerParams(
            dimension_semantics=("arbitrary",),
            vmem_limit_bytes=VMEM_LIMIT_BYTES),
        name="block",
    )(x, *consts)


def kernel(x, meta_tokens, norm_mix_pre, w_in, conv_w, pool_w, pool_scale, w_out,
           norm_mix_post, norm_ffn_pre, w_gate, w_up, w_down, norm_ffn_post):
    return _forward(x, meta_tokens, norm_mix_pre, w_in, conv_w, pool_w, pool_scale,
                    w_out, norm_mix_post, norm_ffn_pre, w_gate, w_up, w_down,
                    norm_ffn_post)
```

```python
import functools

import jax
import jax.numpy as jnp
from jax import lax
from jax.experimental import pallas as pl
from jax.experimental.pallas import tpu as pltpu

RMS_EPS = 1e-6
POOL_WINDOWS = (2, 4, 8, 16)
N_META = 16
SUBLANES = 8
LANES = 128
MXU_COLS = 256
CONV_HIST = SUBLANES
POOL_HIST = 16
SEQ_TILE = 512
POST_ROWS = 16
MIX_IN_LO_AT, MIX_IN_HI_AT, MIX_POOL_AT, MIX_OUT_AT, MIX_POST_AT = 2, 3, 5, 7, 9
DOWN_GROUP = 3
DOWN_LAG = 2
VMEM_LIMIT_BYTES = 56 * 1024 * 1024

_BF16 = jnp.bfloat16
_F32 = jnp.float32


def _rms_norm(x, g):
    y = x * lax.rsqrt(jnp.mean(x * x, axis=-1, keepdims=True) + RMS_EPS)
    return y * g


def _dot(a, b):
    return jnp.dot(a, b, preferred_element_type=_F32)


def _project_in_half(a, win_ref, half):
    n = win_ref.shape[1] // 2
    return _dot(a, win_ref[:, half * n:(half + 1) * n])


def _split_projection(z_lo, z_hi):
    d_conv = z_lo.shape[1] // 2
    return z_lo[:, :d_conv], z_lo[:, d_conv:] * z_hi[:, :d_conv], z_hi[:, d_conv:]


def _project_in(x, g_ref, win_ref):
    a = _rms_norm(x, g_ref[...]).astype(_BF16)
    return _split_projection(_project_in_half(a, win_ref, 0), _project_in_half(a, win_ref, 1))


def _prep_kernel(meta_ref, g_ref, win_ref, poolw_ref, pscale_ref, wout_ref,
                 cv_ref, p_ref, wout_fold_ref):
    _, cv, p = _project_in(meta_ref[...], g_ref, win_ref)
    cv_ref[...] = cv[N_META - CONV_HIST:]
    p_ref[...] = p[N_META - POOL_HIST:]
    d_conv = pscale_ref.shape[1]
    wout_fold_ref[0:d_conv, :] = wout_ref[0:d_conv, :].astype(_BF16)
    for g in range(poolw_ref.shape[0]):
        cols = slice(g * LANES, (g + 1) * LANES)
        rows = slice(d_conv + g * LANES, d_conv + (g + 1) * LANES)
        scaled = poolw_ref[g] * pscale_ref[:, cols]
        wout_fold_ref[rows, :] = jnp.dot(scaled, wout_ref[rows, :],
                                         preferred_element_type=_F32,
                                         precision=lax.Precision.HIGHEST).astype(_BF16)


def _causal_conv(cv_hist, cv, w):
    u = jnp.concatenate([cv_hist, cv], axis=0)
    k = w.shape[0]
    y = w[0:1] * pltpu.roll(u, k - 1, 0)
    for i in range(1, k):
        shifted = pltpu.roll(u, k - 1 - i, 0) if i < k - 1 else u
        y = y + w[i:i + 1] * shifted
    return y[CONV_HIST:]


def _trailing_mean_minus_token(p_hist, p):
    group = p.shape[1] // len(POOL_WINDOWS)
    s = jnp.concatenate([p_hist, p], axis=0)
    means = []
    win = 1
    for target in POOL_WINDOWS:
        while win < target:
            s = s + pltpu.roll(s, win, 0)
            win *= 2
        means.append(s[:, 0:group] * (1.0 / target))
        s = s[:, group:]
    return jnp.concatenate(means, axis=1)[POOL_HIST:] - p


def _mix_conv_pool(b_gate, cv, p, w, cv_hist_ref, p_hist_ref):
    tile = cv.shape[0]
    y_conv = b_gate * _causal_conv(cv_hist_ref[...], cv, w["convw"][...])
    pooled = _trailing_mean_minus_token(p_hist_ref[...], p)
    cv_hist_ref[...] = cv[tile - CONV_HIST:]
    p_hist_ref[...] = p[tile - POOL_HIST:]
    return y_conv.astype(_BF16), pooled.astype(_BF16)


def _zero_after(v):
    bits = lax.bitcast_convert_type(v, jnp.uint32)
    return lax.bitcast_convert_type((bits >> 16) >> 16, _F32)


def _mix_post(x_ref, m_ref, w, h_ref, f_ref):
    g_post, g_pre = w["g_mix_post"][...], w["g_ffn_pre"][...]
    anchor = None
    for r in range(0, m_ref.shape[0], POST_ROWS):
        rows = slice(r, r + POST_ROWS)
        m = m_ref[rows, :]
        ms = jnp.mean(m * m, axis=-1, keepdims=True)
        if anchor is not None:
            ms = ms + anchor
        h = x_ref[0, rows, :] + m * lax.rsqrt(ms + RMS_EPS) * g_post
        rs = lax.rsqrt(jnp.mean(h * h, axis=-1, keepdims=True) + RMS_EPS)
        anchor = _zero_after(rs)
        h_ref[rows, :] = h
        f_ref[rows, :] = (h * rs * g_pre).astype(_BF16)


def _swiglu_act(f, w, j, gu_ref):
    cols = slice(j * MXU_COLS, (j + 1) * MXU_COLS)
    park = gu_ref.at[j % gu_ref.shape[0]]
    park[:, :MXU_COLS] = _dot(f, w["wg"][:, cols])
    park[:, MXU_COLS:] = _dot(f, w["wu"][:, cols])
    gate, up = park[:, :MXU_COLS], park[:, MXU_COLS:]
    return (gate / (1.0 + jnp.exp(-gate)) * up).astype(_BF16)


def _swiglu_down(acts, w, j, acc):
    rows = slice(j * MXU_COLS, (j + len(acts)) * MXU_COLS)
    part = _dot(jnp.concatenate(acts, axis=1), w["wd"][rows, :])
    return part if acc is None else acc + part


_WEIGHT_NAMES = ("mcv", "mp", "g_mix_pre", "win", "convw",
                 "wout", "g_mix_post", "g_ffn_pre", "wg", "wu", "wd", "g_ffn_post")


def _block_kernel(tiles_per_seq, x_ref, *refs):
    w = dict(zip(_WEIGHT_NAMES, refs))
    (o_ref, cv_hist_ref, p_hist_ref, z_ref, gu_ref,
     h_slots_ref, f_slots_ref) = refs[len(_WEIGHT_NAMES):]
    k = pl.program_id(0)
    last = pl.num_programs(0) - 1

    @pl.when(k % tiles_per_seq == 0)
    def _():
        cv_hist_ref[...] = w["mcv"][...]
        p_hist_ref[...] = w["mp"][...]

    n_chunks = w["wd"].shape[0] // MXU_COLS

    def step(h_prev_ref, f_prev_ref, h_next_ref, f_next_ref, mixer=True, swiglu=True):
        x = x_ref[0]
        f_prev = f_prev_ref[...] if swiglu else None
        mix = {}

        def mix_in_lo():
            mix["a"] = _rms_norm(x, w["g_mix_pre"][...]).astype(_BF16)
            half = z_ref.shape[1] // 2
            z_ref[:, :half] = _project_in_half(mix["a"], w["win"], 0)

        def mix_in_hi():
            half = z_ref.shape[1] // 2
            z_ref[:, half:] = _project_in_half(mix.pop("a"), w["win"], 1)

        def mix_conv_pool():
            half = z_ref.shape[1] // 2
            b_gate, cv, p = _split_projection(z_ref[:, :half], z_ref[:, half:])
            mix["y_conv"], mix["pooled"] = _mix_conv_pool(b_gate, cv, p, w, cv_hist_ref, p_hist_ref)

        def mix_out_proj():
            mix_in = jnp.concatenate([mix.pop("y_conv"), mix.pop("pooled")], axis=1)
            h_next_ref[...] = _dot(mix_in, w["wout"][...])

        def mix_post():
            _mix_post(x_ref, h_next_ref, w, h_next_ref, f_next_ref)

        before_chunk = {MIX_IN_LO_AT: mix_in_lo, MIX_IN_HI_AT: mix_in_hi,
                        MIX_POOL_AT: mix_conv_pool, MIX_OUT_AT: mix_out_proj,
                        MIX_POST_AT: mix_post}
        acc = None
        acts = []
        first = 0
        for j in range(n_chunks):
            if mixer and j in before_chunk:
                before_chunk[j]()
            if not swiglu:
                continue
            acts.append(_swiglu_act(f_prev, w, j, gu_ref))
            if len(acts) == DOWN_GROUP + DOWN_LAG:
                acc = _swiglu_down(acts[:DOWN_GROUP], w, first, acc)
                del acts[:DOWN_GROUP]
                first += DOWN_GROUP
        while acts:
            acc = _swiglu_down(acts[:DOWN_GROUP], w, first, acc)
            del acts[:DOWN_GROUP]
            first += DOWN_GROUP
        if swiglu:
            o_ref[0] = h_prev_ref[...] + _rms_norm(acc, w["g_ffn_post"][...])

    nxt = k % 2
    prev = 1 - nxt
    slots = (h_slots_ref.at[prev], f_slots_ref.at[prev], h_slots_ref.at[nxt], f_slots_ref.at[nxt])

    @pl.when(k == 0)
    def _():
        step(*slots, swiglu=False)

    @pl.when((k > 0) & (k < last))
    def _():
        step(*slots)

    @pl.when(k == last)
    def _():
        step(*slots, mixer=False)


def _resident(shape):
    return pl.BlockSpec(shape, lambda *_: (0,) * len(shape),
                        pipeline_mode=pl.Buffered(1))


@functools.partial(jax.jit, static_argnames=("seq_tile",))
def _forward(x, meta_tokens, norm_mix_pre, w_in, conv_w, pool_w, pool_scale,
             w_out, norm_mix_post, norm_ffn_pre, w_gate, w_up, w_down,
             norm_ffn_post, seq_tile=SEQ_TILE):
    bsz, seq, d_model = x.shape
    depth, _, d_in_proj = w_in.shape
    d_conv = d_in_proj // 4
    d_ff = w_gate.shape[2]
    assert depth == 1, "single block only: meta-token outputs are never formed"
    assert meta_tokens.shape == (N_META, d_model)
    assert pool_w.shape[1] == len(POOL_WINDOWS) and pool_w.shape[2] == LANES
    assert pool_scale.shape[1] == d_conv and w_out.shape[1] == 2 * d_conv
    assert seq % seq_tile == 0 and seq_tile % POOL_HIST == 0
    assert d_ff % MXU_COLS == 0

    g_mix_pre = norm_mix_pre.astype(_F32)
    win = w_in[0].astype(_BF16)
    row = lambda v: v.reshape(1, -1).astype(_F32)

    meta_cv, meta_p, wout = pl.pallas_call(
        _prep_kernel,
        out_shape=(jax.ShapeDtypeStruct((CONV_HIST, d_conv), _F32),
                   jax.ShapeDtypeStruct((POOL_HIST, d_conv), _F32),
                   jax.ShapeDtypeStruct((2 * d_conv, d_model), _BF16)),
        name="prep",
    )(meta_tokens.astype(_F32), g_mix_pre, win, pool_w[0].astype(_F32),
      row(pool_scale[0]), w_out[0].astype(_F32))

    consts = (
        meta_cv, meta_p, g_mix_pre, win, conv_w[0].astype(_F32),
        wout, row(norm_mix_post[0]), row(norm_ffn_pre[0]),
        w_gate[0].astype(_BF16), w_up[0].astype(_BF16), w_down[0].astype(_BF16),
        row(norm_ffn_post[0]),
    )
    assert len(consts) == len(_WEIGHT_NAMES)

    tiles_per_seq = seq // seq_tile
    n_tiles = bsz * tiles_per_seq

    def tile_index(t):
        return (t // tiles_per_seq, t % tiles_per_seq, 0)

    x_spec = pl.BlockSpec((1, seq_tile, d_model),
                          lambda k: tile_index(jnp.minimum(k, n_tiles - 1)))
    o_spec = pl.BlockSpec((1, seq_tile, d_model),
                          lambda k: tile_index(jnp.maximum(k - 1, 0)))
    slots = [pltpu.VMEM((2, seq_tile, d_model), _F32), pltpu.VMEM((2, seq_tile, d_model), _BF16)]
    return pl.pallas_call(
        functools.partial(_block_kernel, tiles_per_seq),
        grid=(n_tiles + 1,),
        in_specs=[x_spec] + [_resident(c.shape) for c in consts],
        out_specs=o_spec,
        out_shape=jax.ShapeDtypeStruct(x.shape, x.dtype),
        scratch_shapes=[pltpu.VMEM((CONV_HIST, d_conv), _F32),
                        pltpu.VMEM((POOL_HIST, d_conv), _F32),
                        pltpu.VMEM((seq_tile, d_in_proj), _F32),
                        pltpu.VMEM((2, seq_tile, 2 * MXU_COLS), _F32)] + slots,
        compiler_params=pltpu.CompilerParams(
            dimension_semantics=("arbitrary",),
            vmem_limit_bytes=VMEM_LIMIT_BYTES),
        name="block",
    )(x, *consts)


def kernel(x, meta_tokens, norm_mix_pre, w_in, conv_w, pool_w, pool_scale, w_out,
           norm_mix_post, norm_ffn_pre, w_gate, w_up, w_down, norm_ffn_post):
    return _forward(x, meta_tokens, norm_mix_pre, w_in, conv_w, pool_w, pool_scale,
                    w_out, norm_mix_post, norm_ffn_pre, w_gate, w_up, w_down,
                    norm_ffn_post)
```

```python
import functools

import jax
import jax.numpy as jnp
from jax import lax
from jax.experimental import pallas as pl
from jax.experimental.pallas import tpu as pltpu

RMS_EPS = 1e-6
POOL_WINDOWS = (2, 4, 8, 16)
N_META = 16
SUBLANES = 8
LANES = 128
MXU_COLS = 256
CONV_HIST = SUBLANES
POOL_HIST = 16
SEQ_TILE = 512
POST_ROWS = 16
MIX_IN_LO_AT, MIX_IN_HI_AT, MIX_POOL_AT, MIX_OUT_AT, MIX_POST_AT = 2, 3, 5, 7, 9
DOWN_GROUP = 3
DOWN_LAG = 2
VMEM_LIMIT_BYTES = 60 * 1024 * 1024

_BF16 = jnp.bfloat16
_F32 = jnp.float32


def _rms_norm(x, g):
    y = x * lax.rsqrt(jnp.mean(x * x, axis=-1, keepdims=True) + RMS_EPS)
    return y * g


def _dot(a, b):
    return jnp.dot(a, b, preferred_element_type=_F32)


def _project_in_half(a, win_ref, half):
    n = win_ref.shape[1] // 2
    return _dot(a, win_ref[:, half * n:(half + 1) * n])


def _split_projection(z_lo, z_hi):
    d_conv = z_lo.shape[1] // 2
    return z_lo[:, :d_conv], z_lo[:, d_conv:] * z_hi[:, :d_conv], z_hi[:, d_conv:]


def _project_in(x, g_ref, win_ref):
    a = _rms_norm(x, g_ref[...]).astype(_BF16)
    return _split_projection(_project_in_half(a, win_ref, 0), _project_in_half(a, win_ref, 1))


def _prep_kernel(meta_ref, g_ref, win_ref, poolw_ref, pscale_ref, wout_ref,
                 cv_ref, p_ref, wout_fold_ref):
    _, cv, p = _project_in(meta_ref[...], g_ref, win_ref)
    cv_ref[...] = cv[N_META - CONV_HIST:]
    p_ref[...] = p[N_META - POOL_HIST:]
    d_conv = pscale_ref.shape[1]
    wout_fold_ref[0:d_conv, :] = wout_ref[0:d_conv, :].astype(_BF16)
    for g in range(poolw_ref.shape[0]):
        cols = slice(g * LANES, (g + 1) * LANES)
        rows = slice(d_conv + g * LANES, d_conv + (g + 1) * LANES)
        scaled = poolw_ref[g] * pscale_ref[:, cols]
        wout_fold_ref[rows, :] = jnp.dot(scaled, wout_ref[rows, :],
                                         preferred_element_type=_F32,
                                         precision=lax.Precision.HIGHEST).astype(_BF16)


def _causal_conv(cv_hist, cv, w):
    u = jnp.concatenate([cv_hist, cv], axis=0)
    k = w.shape[0]
    y = w[0:1] * pltpu.roll(u, k - 1, 0)
    for i in range(1, k):
        shifted = pltpu.roll(u, k - 1 - i, 0) if i < k - 1 else u
        y = y + w[i:i + 1] * shifted
    return y[CONV_HIST:]


def _trailing_mean_minus_token(p_hist, p):
    group = p.shape[1] // len(POOL_WINDOWS)
    s = jnp.concatenate([p_hist, p], axis=0)
    means = []
    win = 1
    for target in POOL_WINDOWS:
        while win < target:
            s = s + pltpu.roll(s, win, 0)
            win *= 2
        means.append(s[:, 0:group] * (1.0 / target))
        s = s[:, group:]
    return jnp.concatenate(means, axis=1)[POOL_HIST:] - p


def _mix_conv_pool(b_gate, cv, p, conv_w, cv_hist, p_hist, cv_hist_ref, p_hist_ref):
    tile = cv.shape[0]
    y_conv = b_gate * _causal_conv(cv_hist, cv, conv_w)
    pooled = _trailing_mean_minus_token(p_hist, p)
    cv_hist_ref[...] = cv[tile - CONV_HIST:]
    p_hist_ref[...] = p[tile - POOL_HIST:]
    return y_conv.astype(_BF16), pooled.astype(_BF16)


def _zero_after(v):
    bits = lax.bitcast_convert_type(v, jnp.uint32)
    return lax.bitcast_convert_type((bits >> 16) >> 16, _F32)


def _mix_post(x_ref, m_ref, w, h_ref, f_ref):
    g_post, g_pre = w["g_mix_post"][...], w["g_ffn_pre"][...]
    anchor = None
    for r in range(0, m_ref.shape[0], POST_ROWS):
        rows = slice(r, r + POST_ROWS)
        m = m_ref[rows, :]
        ms = jnp.mean(m * m, axis=-1, keepdims=True)
        if anchor is not None:
            ms = ms + anchor
        h = x_ref[0, rows, :] + m * lax.rsqrt(ms + RMS_EPS) * g_post
        rs = lax.rsqrt(jnp.mean(h * h, axis=-1, keepdims=True) + RMS_EPS)
        anchor = _zero_after(rs)
        h_ref[rows, :] = h
        f_ref[rows, :] = (h * rs * g_pre).astype(_BF16)


def _swiglu_act(f, w, j):
    cols = slice(j * MXU_COLS, (j + 1) * MXU_COLS)
    gate = _dot(f, w["wg"][:, cols])
    up = _dot(f, w["wu"][:, cols])
    return (gate / (1.0 + jnp.exp(-gate)) * up).astype(_BF16)


def _swiglu_down(acts, w, j, acc):
    rows = slice(j * MXU_COLS, (j + len(acts)) * MXU_COLS)
    part = _dot(jnp.concatenate(acts, axis=1), w["wd"][rows, :])
    return part if acc is None else acc + part


_WEIGHT_NAMES = ("mcv", "mp", "g_mix_pre", "win", "convw",
                 "wout", "g_mix_post", "g_ffn_pre", "wg", "wu", "wd", "g_ffn_post")


def _block_kernel(tiles_per_seq, xa_ref, xb_ref, *refs):
    w = dict(zip(_WEIGHT_NAMES, refs))
    o_ref, cv_hist_ref, p_hist_ref, z_ref, h_slots_ref, f_slots_ref = refs[len(_WEIGHT_NAMES):]
    j_step = pl.program_id(0)
    last = pl.num_programs(0) - 1
    tile = xa_ref.shape[1]
    n_chunks = w["wd"].shape[0] // MXU_COLS

    def half_step(x_ref, out_rows, prev, nxt, seq_start=False, mixer=True, swiglu=True):
        if swiglu:
            h_prev_ref, f_prev_ref = h_slots_ref.at[prev], f_slots_ref.at[prev]
            f_prev = f_prev_ref[...]
        if mixer:
            h_next_ref, f_next_ref = h_slots_ref.at[nxt], f_slots_ref.at[nxt]
            x = x_ref[0]
        mix = {}

        def carry(hist_ref, meta_ref):
            if seq_start is True:
                return meta_ref[...]
            if seq_start is False:
                return hist_ref[...]
            return jnp.where(seq_start, meta_ref[...], hist_ref[...])

        def mix_in_lo():
            mix["a"] = _rms_norm(x, w["g_mix_pre"][...]).astype(_BF16)
            half = z_ref.shape[1] // 2
            z_ref[:, :half] = _project_in_half(mix["a"], w["win"], 0)

        def mix_in_hi():
            half = z_ref.shape[1] // 2
            z_ref[:, half:] = _project_in_half(mix.pop("a"), w["win"], 1)

        def mix_conv_pool():
            half = z_ref.shape[1] // 2
            b_gate, cv, p = _split_projection(z_ref[:, :half], z_ref[:, half:])
            mix["y_conv"], mix["pooled"] = _mix_conv_pool(
                b_gate, cv, p, w["convw"][...], carry(cv_hist_ref, w["mcv"]),
                carry(p_hist_ref, w["mp"]), cv_hist_ref, p_hist_ref)

        def mix_out_proj():
            mix_in = jnp.concatenate([mix.pop("y_conv"), mix.pop("pooled")], axis=1)
            h_next_ref[...] = _dot(mix_in, w["wout"][...])

        def mix_post():
            _mix_post(x_ref, h_next_ref, w, h_next_ref, f_next_ref)

        before_chunk = {MIX_IN_LO_AT: mix_in_lo, MIX_IN_HI_AT: mix_in_hi,
                        MIX_POOL_AT: mix_conv_pool, MIX_OUT_AT: mix_out_proj,
                        MIX_POST_AT: mix_post}
        acc = None
        acts = []
        first = 0
        for j in range(n_chunks):
            if mixer and j in before_chunk:
                before_chunk[j]()
            if not swiglu:
                continue
            acts.append(_swiglu_act(f_prev, w, j))
            if len(acts) == DOWN_GROUP + DOWN_LAG:
                acc = _swiglu_down(acts[:DOWN_GROUP], w, first, acc)
                del acts[:DOWN_GROUP]
                first += DOWN_GROUP
        while acts:
            acc = _swiglu_down(acts[:DOWN_GROUP], w, first, acc)
            del acts[:DOWN_GROUP]
            first += DOWN_GROUP
        if swiglu:
            o_ref[0, out_rows, :] = h_prev_ref[...] + _rms_norm(acc, w["g_ffn_post"][...])

    first_rows, second_rows = slice(0, tile), slice(tile, 2 * tile)

    @pl.when(j_step == 0)
    def _():
        half_step(xb_ref, None, None, 1, seq_start=True, swiglu=False)

    @pl.when((j_step > 0) & (j_step < last))
    def _():
        half_step(xa_ref, first_rows, 1, 0)
        half_step(xb_ref, second_rows, 0, 1, seq_start=(2 * j_step) % tiles_per_seq == 0)

    @pl.when(j_step == last)
    def _():
        half_step(xa_ref, first_rows, 1, 0)
        half_step(None, second_rows, 0, None, mixer=False)


def _resident(shape):
    return pl.BlockSpec(shape, lambda *_: (0,) * len(shape),
                        pipeline_mode=pl.Buffered(1))


@functools.partial(jax.jit, static_argnames=("seq_tile",))
def _forward(x, meta_tokens, norm_mix_pre, w_in, conv_w, pool_w, pool_scale,
             w_out, norm_mix_post, norm_ffn_pre, w_gate, w_up, w_down,
             norm_ffn_post, seq_tile=SEQ_TILE):
    bsz, seq, d_model = x.shape
    depth, _, d_in_proj = w_in.shape
    d_conv = d_in_proj // 4
    d_ff = w_gate.shape[2]
    assert depth == 1, "single block only: meta-token outputs are never formed"
    assert meta_tokens.shape == (N_META, d_model)
    assert pool_w.shape[1] == len(POOL_WINDOWS) and pool_w.shape[2] == LANES
    assert pool_scale.shape[1] == d_conv and w_out.shape[1] == 2 * d_conv
    assert seq % seq_tile == 0 and seq_tile % POOL_HIST == 0
    assert d_ff % MXU_COLS == 0

    g_mix_pre = norm_mix_pre.astype(_F32)
    win = w_in[0].astype(_BF16)
    row = lambda v: v.reshape(1, -1).astype(_F32)

    meta_cv, meta_p, wout = pl.pallas_call(
        _prep_kernel,
        out_shape=(jax.ShapeDtypeStruct((CONV_HIST, d_conv), _F32),
                   jax.ShapeDtypeStruct((POOL_HIST, d_conv), _F32),
                   jax.ShapeDtypeStruct((2 * d_conv, d_model), _BF16)),
        name="prep",
    )(meta_tokens.astype(_F32), g_mix_pre, win, pool_w[0].astype(_F32),
      row(pool_scale[0]), w_out[0].astype(_F32))

    consts = (
        meta_cv, meta_p, g_mix_pre, win, conv_w[0].astype(_F32),
        wout, row(norm_mix_post[0]), row(norm_ffn_pre[0]),
        w_gate[0].astype(_BF16), w_up[0].astype(_BF16), w_down[0].astype(_BF16),
        row(norm_ffn_post[0]),
    )
    assert len(consts) == len(_WEIGHT_NAMES)

    tiles_per_seq = seq // seq_tile
    n_tiles = bsz * tiles_per_seq
    assert tiles_per_seq % 2 == 0
    pairs_per_seq = tiles_per_seq // 2

    def tile_index(t):
        t = jnp.clip(t, 0, n_tiles - 1)
        return (t // tiles_per_seq, t % tiles_per_seq, 0)

    def pair_index(p):
        p = jnp.maximum(p, 0)
        return (p // pairs_per_seq, p % pairs_per_seq, 0)

    xa_spec = pl.BlockSpec((1, seq_tile, d_model), lambda j: tile_index(2 * j - 1))
    xb_spec = pl.BlockSpec((1, seq_tile, d_model), lambda j: tile_index(2 * j))
    o_spec = pl.BlockSpec((1, 2 * seq_tile, d_model), lambda j: pair_index(j - 1))
    slots = [pltpu.VMEM((2, seq_tile, d_model), _F32), pltpu.VMEM((2, seq_tile, d_model), _BF16)]
    return pl.pallas_call(
        functools.partial(_block_kernel, tiles_per_seq),
        grid=(n_tiles // 2 + 1,),
        in_specs=[xa_spec, xb_spec] + [_resident(c.shape) for c in consts],
        out_specs=o_spec,
        out_shape=jax.ShapeDtypeStruct(x.shape, x.dtype),
        scratch_shapes=[pltpu.VMEM((CONV_HIST, d_conv), _F32),
                        pltpu.VMEM((POOL_HIST, d_conv), _F32),
                        pltpu.VMEM((seq_tile, d_in_proj), _F32)] + slots,
        compiler_params=pltpu.CompilerParams(
            dimension_semantics=("arbitrary",),
            vmem_limit_bytes=VMEM_LIMIT_BYTES),
        name="block",
    )(x, x, *consts)


def kernel(x, meta_tokens, norm_mix_pre, w_in, conv_w, pool_w, pool_scale, w_out,
           norm_mix_post, norm_ffn_pre, w_gate, w_up, w_down, norm_ffn_post):
    return _forward(x, meta_tokens, norm_mix_pre, w_in, conv_w, pool_w, pool_scale,
                    w_out, norm_mix_post, norm_ffn_pre, w_gate, w_up, w_down,
                    norm_ffn_post)
```

```python
import functools

import jax
import jax.numpy as jnp
from jax import lax
from jax.experimental import pallas as pl
from jax.experimental.pallas import tpu as pltpu

RMS_EPS = 1e-6
POOL_WINDOWS = (2, 4, 8, 16)
N_META = 16
SUBLANES = 8
LANES = 128
MXU_COLS = 256
CONV_HIST = SUBLANES
POOL_HIST = 16
SEQ_TILE = 512
POST_ROWS = 16
MIX_IN_LO_AT, MIX_IN_HI_AT, MIX_POOL_AT, MIX_OUT_AT, MIX_POST_AT = 2, 3, 5, 7, 9
DOWN_GROUP = 3
DOWN_LAG = 2
VMEM_LIMIT_BYTES = 56 * 1024 * 1024

_BF16 = jnp.bfloat16
_F32 = jnp.float32


def _rms_norm(x, g):
    y = x * lax.rsqrt(jnp.mean(x * x, axis=-1, keepdims=True) + RMS_EPS)
    return y * g


def _dot(a, b):
    return jnp.dot(a, b, preferred_element_type=_F32)


def _project_in_half(a, win_ref, half):
    n = win_ref.shape[1] // 2
    return _dot(a, win_ref[:, half * n:(half + 1) * n])


def _split_projection(z_lo, z_hi):
    d_conv = z_lo.shape[1] // 2
    return z_lo[:, :d_conv], z_lo[:, d_conv:] * z_hi[:, :d_conv], z_hi[:, d_conv:]


def _project_in(x, g_ref, win_ref):
    a = _rms_norm(x, g_ref[...]).astype(_BF16)
    return _split_projection(_project_in_half(a, win_ref, 0), _project_in_half(a, win_ref, 1))


def _prep_kernel(meta_ref, g_ref, win_ref, poolw_ref, pscale_ref, wout_ref,
                 cv_ref, p_ref, wout_fold_ref):
    _, cv, p = _project_in(meta_ref[...], g_ref, win_ref)
    cv_ref[...] = cv[N_META - CONV_HIST:]
    p_ref[...] = p[N_META - POOL_HIST:]
    d_conv = pscale_ref.shape[1]
    wout_fold_ref[0:d_conv, :] = wout_ref[0:d_conv, :].astype(_BF16)
    for g in range(poolw_ref.shape[0]):
        cols = slice(g * LANES, (g + 1) * LANES)
        rows = slice(d_conv + g * LANES, d_conv + (g + 1) * LANES)
        scaled = poolw_ref[g] * pscale_ref[:, cols]
        wout_fold_ref[rows, :] = jnp.dot(scaled, wout_ref[rows, :],
                                         preferred_element_type=_F32,
                                         precision=lax.Precision.HIGHEST).astype(_BF16)


def _causal_conv(cv_hist, cv, w):
    u = jnp.concatenate([cv_hist, cv], axis=0)
    k = w.shape[0]
    y = w[0:1] * pltpu.roll(u, k - 1, 0)
    for i in range(1, k):
        shifted = pltpu.roll(u, k - 1 - i, 0) if i < k - 1 else u
        y = y + w[i:i + 1] * shifted
    return y[CONV_HIST:]


def _trailing_mean_minus_token(p_hist, p):
    group = p.shape[1] // len(POOL_WINDOWS)
    s = jnp.concatenate([p_hist, p], axis=0)
    means = []
    win = 1
    for target in POOL_WINDOWS:
        while win < target:
            s = s + pltpu.roll(s, win, 0)
            win *= 2
        means.append(s[:, 0:group] * (1.0 / target))
        s = s[:, group:]
    return jnp.concatenate(means, axis=1)[POOL_HIST:] - p


def _mix_conv_pool(b_gate, cv, p, w, cv_hist_ref, p_hist_ref):
    tile = cv.shape[0]
    y_conv = b_gate * _causal_conv(cv_hist_ref[...], cv, w["convw"][...])
    pooled = _trailing_mean_minus_token(p_hist_ref[...], p)
    cv_hist_ref[...] = cv[tile - CONV_HIST:]
    p_hist_ref[...] = p[tile - POOL_HIST:]
    return y_conv.astype(_BF16), pooled.astype(_BF16)


def _zero_after(v):
    bits = lax.bitcast_convert_type(v, jnp.uint32)
    return lax.bitcast_convert_type((bits >> 16) >> 16, _F32)


def _mix_post(x_ref, m_ref, w, h_ref, f_ref):
    g_post, g_pre = w["g_mix_post"][...], w["g_ffn_pre"][...]
    anchor = None
    for r in range(0, m_ref.shape[0], POST_ROWS):
        rows = slice(r, r + POST_ROWS)
        m = m_ref[rows, :]
        ms = jnp.mean(m * m, axis=-1, keepdims=True)
        if anchor is not None:
            ms = ms + anchor
        h = x_ref[0, rows, :] + m * lax.rsqrt(ms + RMS_EPS) * g_post
        rs = lax.rsqrt(jnp.mean(h * h, axis=-1, keepdims=True) + RMS_EPS)
        anchor = _zero_after(rs)
        h_ref[rows, :] = h
        f_ref[rows, :] = (h * rs * g_pre).astype(_BF16)


def _swiglu_act(f, w, j):
    cols = slice(j * MXU_COLS, (j + 1) * MXU_COLS)
    gate = _dot(f, w["wg"][:, cols])
    up = _dot(f, w["wu"][:, cols])
    return (gate / (1.0 + jnp.exp(-gate)) * up).astype(_BF16)


def _swiglu_down(acts, w, j, acc):
    rows = slice(j * MXU_COLS, (j + len(acts)) * MXU_COLS)
    part = _dot(jnp.concatenate(acts, axis=1), w["wd"][rows, :])
    return part if acc is None else acc + part


_WEIGHT_NAMES = ("mcv", "mp", "g_mix_pre", "win", "convw",
                 "wout", "g_mix_post", "g_ffn_pre", "g_ffn_post")
_HBM_WEIGHT_NAMES = ("wg", "wu", "wd")
CAST_ROWS = {"wg": 64, "wu": 64, "wd": 176}


def _cast_to_vmem(src_hbm, dst_ref, stage_ref, sem_ref, rows):
    n_rows, n_cols = src_hbm.shape
    stage = lambda i: stage_ref.at[i % 2, pl.ds(0, rows), pl.ds(0, n_cols)]

    def copy(i):
        return pltpu.make_async_copy(src_hbm.at[pl.ds(i * rows, rows), :], stage(i), sem_ref.at[i % 2])

    n = n_rows // rows
    copy(0).start()
    for i in range(n):
        if i + 1 < n:
            copy(i + 1).start()
        copy(i).wait()
        dst_ref[i * rows:(i + 1) * rows, :] = stage(i)[...].astype(_BF16)


def _block_kernel(tiles_per_seq, x_ref, *refs):
    w = dict(zip(_WEIGHT_NAMES, refs))
    hbm = dict(zip(_HBM_WEIGHT_NAMES, refs[len(_WEIGHT_NAMES):]))
    (o_ref, cv_hist_ref, p_hist_ref, z_ref, h_slots_ref, f_slots_ref,
     w["wg"], w["wu"], w["wd"], stage_ref, sem_ref) = refs[len(_WEIGHT_NAMES) + len(_HBM_WEIGHT_NAMES):]
    k = pl.program_id(0)
    last = pl.num_programs(0) - 1

    @pl.when(k % tiles_per_seq == 0)
    def _():
        cv_hist_ref[...] = w["mcv"][...]
        p_hist_ref[...] = w["mp"][...]

    n_chunks = w["wd"].shape[0] // MXU_COLS

    def step(h_prev_ref, f_prev_ref, h_next_ref, f_next_ref, mixer=True, swiglu=True):
        x = x_ref[0]
        f_prev = f_prev_ref[...] if swiglu else None
        mix = {}

        def mix_in_lo():
            mix["a"] = _rms_norm(x, w["g_mix_pre"][...]).astype(_BF16)
            half = z_ref.shape[1] // 2
            z_ref[:, :half] = _project_in_half(mix["a"], w["win"], 0)

        def mix_in_hi():
            half = z_ref.shape[1] // 2
            z_ref[:, half:] = _project_in_half(mix.pop("a"), w["win"], 1)

        def mix_conv_pool():
            half = z_ref.shape[1] // 2
            b_gate, cv, p = _split_projection(z_ref[:, :half], z_ref[:, half:])
            mix["y_conv"], mix["pooled"] = _mix_conv_pool(b_gate, cv, p, w, cv_hist_ref, p_hist_ref)

        def mix_out_proj():
            mix_in = jnp.concatenate([mix.pop("y_conv"), mix.pop("pooled")], axis=1)
            h_next_ref[...] = _dot(mix_in, w["wout"][...])

        def mix_post():
            _mix_post(x_ref, h_next_ref, w, h_next_ref, f_next_ref)

        before_chunk = {MIX_IN_LO_AT: mix_in_lo, MIX_IN_HI_AT: mix_in_hi,
                        MIX_POOL_AT: mix_conv_pool, MIX_OUT_AT: mix_out_proj,
                        MIX_POST_AT: mix_post}
        acc = None
        acts = []
        first = 0
        for j in range(n_chunks):
            if mixer and j in before_chunk:
                before_chunk[j]()
            if not swiglu:
                continue
            acts.append(_swiglu_act(f_prev, w, j))
            if len(acts) == DOWN_GROUP + DOWN_LAG:
                acc = _swiglu_down(acts[:DOWN_GROUP], w, first, acc)
                del acts[:DOWN_GROUP]
                first += DOWN_GROUP
        while acts:
            acc = _swiglu_down(acts[:DOWN_GROUP], w, first, acc)
            del acts[:DOWN_GROUP]
            first += DOWN_GROUP
        if swiglu:
            o_ref[0] = h_prev_ref[...] + _rms_norm(acc, w["g_ffn_post"][...])

    nxt = k % 2
    prev = 1 - nxt
    slots = (h_slots_ref.at[prev], f_slots_ref.at[prev], h_slots_ref.at[nxt], f_slots_ref.at[nxt])

    @pl.when(k == 0)
    def _():
        for name in _HBM_WEIGHT_NAMES:
            _cast_to_vmem(hbm[name], w[name], stage_ref, sem_ref, CAST_ROWS[name])
        step(*slots, swiglu=False)

    @pl.when((k > 0) & (k < last))
    def _():
        step(*slots)

    @pl.when(k == last)
    def _():
        step(*slots, mixer=False)


def _resident(shape):
    return pl.BlockSpec(shape, lambda *_: (0,) * len(shape),
                        pipeline_mode=pl.Buffered(1))


@functools.partial(jax.jit, static_argnames=("seq_tile",))
def _forward(x, meta_tokens, norm_mix_pre, w_in, conv_w, pool_w, pool_scale,
             w_out, norm_mix_post, norm_ffn_pre, w_gate, w_up, w_down,
             norm_ffn_post, seq_tile=SEQ_TILE):
    bsz, seq, d_model = x.shape
    depth, _, d_in_proj = w_in.shape
    d_conv = d_in_proj // 4
    d_ff = w_gate.shape[2]
    assert depth == 1, "single block only: meta-token outputs are never formed"
    assert meta_tokens.shape == (N_META, d_model)
    assert pool_w.shape[1] == len(POOL_WINDOWS) and pool_w.shape[2] == LANES
    assert pool_scale.shape[1] == d_conv and w_out.shape[1] == 2 * d_conv
    assert seq % seq_tile == 0 and seq_tile % POOL_HIST == 0
    assert d_ff % MXU_COLS == 0

    g_mix_pre = norm_mix_pre.astype(_F32)
    win = w_in[0].astype(_BF16)
    row = lambda v: v.reshape(1, -1).astype(_F32)

    meta_cv, meta_p, wout = pl.pallas_call(
        _prep_kernel,
        out_shape=(jax.ShapeDtypeStruct((CONV_HIST, d_conv), _F32),
                   jax.ShapeDtypeStruct((POOL_HIST, d_conv), _F32),
                   jax.ShapeDtypeStruct((2 * d_conv, d_model), _BF16)),
        name="prep",
    )(meta_tokens.astype(_F32), g_mix_pre, win, pool_w[0].astype(_F32),
      row(pool_scale[0]), w_out[0].astype(_F32))

    consts = (
        meta_cv, meta_p, g_mix_pre, win, conv_w[0].astype(_F32),
        wout, row(norm_mix_post[0]), row(norm_ffn_pre[0]), row(norm_ffn_post[0]),
    )
    assert len(consts) == len(_WEIGHT_NAMES)
    hbm_weights = (w_gate[0].astype(_F32), w_up[0].astype(_F32), w_down[0].astype(_F32))
    stage_cols = max(m.shape[1] for m in hbm_weights)
    stage_rows = max(CAST_ROWS.values())
    assert all(m.shape[0] % CAST_ROWS[n] == 0 for n, m in zip(_HBM_WEIGHT_NAMES, hbm_weights))

    tiles_per_seq = seq // seq_tile
    n_tiles = bsz * tiles_per_seq

    def tile_index(t):
        return (t // tiles_per_seq, t % tiles_per_seq, 0)

    x_spec = pl.BlockSpec((1, seq_tile, d_model),
                          lambda k: tile_index(jnp.minimum(k, n_tiles - 1)))
    o_spec = pl.BlockSpec((1, seq_tile, d_model),
                          lambda k: tile_index(jnp.maximum(k - 1, 0)))
    slots = [pltpu.VMEM((2, seq_tile, d_model), _F32), pltpu.VMEM((2, seq_tile, d_model), _BF16)]
    return pl.pallas_call(
        functools.partial(_block_kernel, tiles_per_seq),
        grid=(n_tiles + 1,),
        in_specs=([x_spec] + [_resident(c.shape) for c in consts]
                  + [pl.BlockSpec(memory_space=pl.ANY)] * len(hbm_weights)),
        out_specs=o_spec,
        out_shape=jax.ShapeDtypeStruct(x.shape, x.dtype),
        scratch_shapes=[pltpu.VMEM((CONV_HIST, d_conv), _F32),
                        pltpu.VMEM((POOL_HIST, d_conv), _F32),
                        pltpu.VMEM((seq_tile, d_in_proj), _F32)] + slots
                       + [pltpu.VMEM(m.shape, _BF16) for m in hbm_weights]
                       + [pltpu.VMEM((2, stage_rows, stage_cols), _F32),
                          pltpu.SemaphoreType.DMA((2,))],
        compiler_params=pltpu.CompilerParams(
            dimension_semantics=("arbitrary",),
            vmem_limit_bytes=VMEM_LIMIT_BYTES),
        name="block",
    )(x, *consts, *hbm_weights)


def kernel(x, meta_tokens, norm_mix_pre, w_in, conv_w, pool_w, pool_scale, w_out,
           norm_mix_post, norm_ffn_pre, w_gate, w_up, w_down, norm_ffn_post):
    return _forward(x, meta_tokens, norm_mix_pre, w_in, conv_w, pool_w, pool_scale,
                    w_out, norm_mix_post, norm_ffn_pre, w_gate, w_up, w_down,
                    norm_ffn_post)
```

```python
import functools

import jax
import jax.numpy as jnp
from jax import lax
from jax.experimental import pallas as pl
from jax.experimental.pallas import tpu as pltpu

RMS_EPS = 1e-6
POOL_WINDOWS = (2, 4, 8, 16)
N_META = 16
SUBLANES = 8
LANES = 128
MXU_COLS = 256
CONV_HIST = SUBLANES
POOL_HIST = 16
SEQ_TILE = 512
POST_ROWS = 16
MIX_IN_LO_AT, MIX_IN_HI_AT, MIX_POOL_AT, MIX_OUT_AT, MIX_POST_AT = 2, 3, 5, 7, 9
DOWN_GROUP = 3
DOWN_LAG = 2
VMEM_LIMIT_BYTES = 56 * 1024 * 1024

_BF16 = jnp.bfloat16
_F32 = jnp.float32


def _rms_norm(x, g):
    y = x * lax.rsqrt(jnp.mean(x * x, axis=-1, keepdims=True) + RMS_EPS)
    return y * g


def _dot(a, b):
    return jnp.dot(a, b, preferred_element_type=_F32)


def _project_in_half(a, win_ref, half):
    n = win_ref.shape[1] // 2
    return _dot(a, win_ref[:, half * n:(half + 1) * n])


def _split_projection(z_lo, z_hi):
    d_conv = z_lo.shape[1] // 2
    return z_lo[:, :d_conv], z_lo[:, d_conv:] * z_hi[:, :d_conv], z_hi[:, d_conv:]


def _project_in(x, g_ref, win_ref):
    a = _rms_norm(x, g_ref[...]).astype(_BF16)
    return _split_projection(_project_in_half(a, win_ref, 0), _project_in_half(a, win_ref, 1))


def _prep_kernel(meta_ref, g_ref, win_ref, poolw_ref, pscale_ref, wout_ref,
                 cv_ref, p_ref, wout_fold_ref):
    _, cv, p = _project_in(meta_ref[...], g_ref, win_ref)
    cv_ref[...] = cv[N_META - CONV_HIST:]
    p_ref[...] = p[N_META - POOL_HIST:]
    d_conv = pscale_ref.shape[1]
    wout_fold_ref[0:d_conv, :] = wout_ref[0:d_conv, :].astype(_BF16)
    for g in range(poolw_ref.shape[0]):
        cols = slice(g * LANES, (g + 1) * LANES)
        rows = slice(d_conv + g * LANES, d_conv + (g + 1) * LANES)
        scaled = poolw_ref[g] * pscale_ref[:, cols]
        wout_fold_ref[rows, :] = jnp.dot(scaled, wout_ref[rows, :],
                                         preferred_element_type=_F32,
                                         precision=lax.Precision.HIGHEST).astype(_BF16)


def _causal_conv(cv_hist, cv, w):
    u = jnp.concatenate([cv_hist, cv], axis=0)
    k = w.shape[0]
    y = w[0:1] * pltpu.roll(u, k - 1, 0)
    for i in range(1, k):
        shifted = pltpu.roll(u, k - 1 - i, 0) if i < k - 1 else u
        y = y + w[i:i + 1] * shifted
    return y[CONV_HIST:]


def _trailing_mean_minus_token(p_hist, p):
    group = p.shape[1] // len(POOL_WINDOWS)
    s = jnp.concatenate([p_hist, p], axis=0)
    means = []
    win = 1
    for target in POOL_WINDOWS:
        while win < target:
            s = s + pltpu.roll(s, win, 0)
            win *= 2
        means.append(s[:, 0:group] * (1.0 / target))
        s = s[:, group:]
    return jnp.concatenate(means, axis=1)[POOL_HIST:] - p


def _mix_conv_pool(b_gate, cv, p, w, cv_hist_ref, p_hist_ref):
    tile = cv.shape[0]
    y_conv = b_gate * _causal_conv(cv_hist_ref[...], cv, w["convw"][...])
    pooled = _trailing_mean_minus_token(p_hist_ref[...], p)
    cv_hist_ref[...] = cv[tile - CONV_HIST:]
    p_hist_ref[...] = p[tile - POOL_HIST:]
    return y_conv.astype(_BF16), pooled.astype(_BF16)


def _zero_after(v):
    bits = lax.bitcast_convert_type(v, jnp.uint32)
    return lax.bitcast_convert_type((bits >> 16) >> 16, _F32)


def _mix_post(x_ref, m_ref, w, h_ref, f_ref):
    g_post, g_pre = w["g_mix_post"][...], w["g_ffn_pre"][...]
    anchor = None
    for r in range(0, m_ref.shape[0], POST_ROWS):
        rows = slice(r, r + POST_ROWS)
        m = m_ref[rows, :]
        ms = jnp.mean(m * m, axis=-1, keepdims=True)
        if anchor is not None:
            ms = ms + anchor
        h = x_ref[0, rows, :] + m * lax.rsqrt(ms + RMS_EPS) * g_post
        rs = lax.rsqrt(jnp.mean(h * h, axis=-1, keepdims=True) + RMS_EPS)
        anchor = _zero_after(rs)
        h_ref[rows, :] = h
        f_ref[rows, :] = (h * rs * g_pre).astype(_BF16)


def _swiglu_act(f, w, j):
    cols = slice(j * MXU_COLS, (j + 1) * MXU_COLS)
    gate = _dot(f, w["wg"][:, cols])
    up = _dot(f, w["wu"][:, cols])
    return (gate / (1.0 + jnp.exp(-gate)) * up).astype(_BF16)


def _swiglu_down(acts, w, j, acc):
    rows = slice(j * MXU_COLS, (j + len(acts)) * MXU_COLS)
    part = _dot(jnp.concatenate(acts, axis=1), w["wd"][rows, :])
    return part if acc is None else acc + part


_WEIGHT_NAMES = ("mcv", "mp", "g_mix_pre", "win", "convw",
                 "wout", "g_mix_post", "g_ffn_pre", "g_ffn_post")
_HBM_WEIGHT_NAMES = ("wg", "wu", "wd")
CAST_ROWS = 128
CAST_STAGES = 4


def _cast_to_vmem(jobs, stage_ref, sem_ref):
    chunks = [(src, dst, r) for src, dst in jobs for r in range(0, src.shape[0], CAST_ROWS)]
    depth = stage_ref.shape[0]

    def stage(c):
        return stage_ref.at[c % depth, :, pl.ds(0, chunks[c][0].shape[1])]

    def copy(c):
        src, _, r = chunks[c]
        return pltpu.make_async_copy(src.at[pl.ds(r, CAST_ROWS), :], stage(c), sem_ref.at[c % depth])

    for c in range(min(depth - 1, len(chunks))):
        copy(c).start()
    for c, (_, dst, r) in enumerate(chunks):
        if c + depth - 1 < len(chunks):
            copy(c + depth - 1).start()
        copy(c).wait()
        dst[r:r + CAST_ROWS, :] = stage(c)[...].astype(_BF16)


def _block_kernel(tiles_per_seq, x_ref, *refs):
    w = dict(zip(_WEIGHT_NAMES, refs))
    hbm = dict(zip(_HBM_WEIGHT_NAMES, refs[len(_WEIGHT_NAMES):]))
    (o_ref, cv_hist_ref, p_hist_ref, z_ref, h_slots_ref, f_slots_ref,
     w["wg"], w["wu"], w["wd"], stage_ref, sem_ref) = refs[len(_WEIGHT_NAMES) + len(_HBM_WEIGHT_NAMES):]
    k = pl.program_id(0)
    last = pl.num_programs(0) - 1

    @pl.when(k % tiles_per_seq == 0)
    def _():
        cv_hist_ref[...] = w["mcv"][...]
        p_hist_ref[...] = w["mp"][...]

    n_chunks = w["wd"].shape[0] // MXU_COLS

    def step(h_prev_ref, f_prev_ref, h_next_ref, f_next_ref, mixer=True, swiglu=True):
        x = x_ref[0]
        f_prev = f_prev_ref[...] if swiglu else None
        mix = {}

        def mix_in_lo():
            mix["a"] = _rms_norm(x, w["g_mix_pre"][...]).astype(_BF16)
            half = z_ref.shape[1] // 2
            z_ref[:, :half] = _project_in_half(mix["a"], w["win"], 0)

        def mix_in_hi():
            half = z_ref.shape[1] // 2
            z_ref[:, half:] = _project_in_half(mix.pop("a"), w["win"], 1)

        def mix_conv_pool():
            half = z_ref.shape[1] // 2
            b_gate, cv, p = _split_projection(z_ref[:, :half], z_ref[:, half:])
            mix["y_conv"], mix["pooled"] = _mix_conv_pool(b_gate, cv, p, w, cv_hist_ref, p_hist_ref)

        def mix_out_proj():
            mix_in = jnp.concatenate([mix.pop("y_conv"), mix.pop("pooled")], axis=1)
            h_next_ref[...] = _dot(mix_in, w["wout"][...])

        def mix_post():
            _mix_post(x_ref, h_next_ref, w, h_next_ref, f_next_ref)

        before_chunk = {MIX_IN_LO_AT: mix_in_lo, MIX_IN_HI_AT: mix_in_hi,
                        MIX_POOL_AT: mix_conv_pool, MIX_OUT_AT: mix_out_proj,
                        MIX_POST_AT: mix_post}
        acc = None
        acts = []
        first = 0
        for j in range(n_chunks):
            if mixer and j in before_chunk:
                before_chunk[j]()
            if not swiglu:
                continue
            acts.append(_swiglu_act(f_prev, w, j))
            if len(acts) == DOWN_GROUP + DOWN_LAG:
                acc = _swiglu_down(acts[:DOWN_GROUP], w, first, acc)
                del acts[:DOWN_GROUP]
                first += DOWN_GROUP
        while acts:
            acc = _swiglu_down(acts[:DOWN_GROUP], w, first, acc)
            del acts[:DOWN_GROUP]
            first += DOWN_GROUP
        if swiglu:
            o_ref[0] = h_prev_ref[...] + _rms_norm(acc, w["g_ffn_post"][...])

    nxt = k % 2
    prev = 1 - nxt
    slots = (h_slots_ref.at[prev], f_slots_ref.at[prev], h_slots_ref.at[nxt], f_slots_ref.at[nxt])

    @pl.when(k == 0)
    def _():
        _cast_to_vmem([(hbm[name], w[name]) for name in _HBM_WEIGHT_NAMES], stage_ref, sem_ref)
        step(*slots, swiglu=False)

    @pl.when((k > 0) & (k < last))
    def _():
        step(*slots)

    @pl.when(k == last)
    def _():
        step(*slots, mixer=False)


def _resident(shape):
    return pl.BlockSpec(shape, lambda *_: (0,) * len(shape),
                        pipeline_mode=pl.Buffered(1))


@functools.partial(jax.jit, static_argnames=("seq_tile",))
def _forward(x, meta_tokens, norm_mix_pre, w_in, conv_w, pool_w, pool_scale,
             w_out, norm_mix_post, norm_ffn_pre, w_gate, w_up, w_down,
             norm_ffn_post, seq_tile=SEQ_TILE):
    bsz, seq, d_model = x.shape
    depth, _, d_in_proj = w_in.shape
    d_conv = d_in_proj // 4
    d_ff = w_gate.shape[2]
    assert depth == 1, "single block only: meta-token outputs are never formed"
    assert meta_tokens.shape == (N_META, d_model)
    assert pool_w.shape[1] == len(POOL_WINDOWS) and pool_w.shape[2] == LANES
    assert pool_scale.shape[1] == d_conv and w_out.shape[1] == 2 * d_conv
    assert seq % seq_tile == 0 and seq_tile % POOL_HIST == 0
    assert d_ff % MXU_COLS == 0

    g_mix_pre = norm_mix_pre.astype(_F32)
    win = w_in[0].astype(_BF16)
    row = lambda v: v.reshape(1, -1).astype(_F32)

    meta_cv, meta_p, wout = pl.pallas_call(
        _prep_kernel,
        out_shape=(jax.ShapeDtypeStruct((CONV_HIST, d_conv), _F32),
                   jax.ShapeDtypeStruct((POOL_HIST, d_conv), _F32),
                   jax.ShapeDtypeStruct((2 * d_conv, d_model), _BF16)),
        name="prep",
    )(meta_tokens.astype(_F32), g_mix_pre, win, pool_w[0].astype(_F32),
      row(pool_scale[0]), w_out[0].astype(_F32))

    consts = (
        meta_cv, meta_p, g_mix_pre, win, conv_w[0].astype(_F32),
        wout, row(norm_mix_post[0]), row(norm_ffn_pre[0]), row(norm_ffn_post[0]),
    )
    assert len(consts) == len(_WEIGHT_NAMES)
    hbm_weights = (w_gate[0].astype(_F32), w_up[0].astype(_F32), w_down[0].astype(_F32))
    stage_cols = max(m.shape[1] for m in hbm_weights)
    assert all(m.shape[0] % CAST_ROWS == 0 for m in hbm_weights)

    tiles_per_seq = seq // seq_tile
    n_tiles = bsz * tiles_per_seq

    def tile_index(t):
        return (t // tiles_per_seq, t % tiles_per_seq, 0)

    x_spec = pl.BlockSpec((1, seq_tile, d_model),
                          lambda k: tile_index(jnp.minimum(k, n_tiles - 1)))
    o_spec = pl.BlockSpec((1, seq_tile, d_model),
                          lambda k: tile_index(jnp.maximum(k - 1, 0)))
    slots = [pltpu.VMEM((2, seq_tile, d_model), _F32), pltpu.VMEM((2, seq_tile, d_model), _BF16)]
    return pl.pallas_call(
        functools.partial(_block_kernel, tiles_per_seq),
        grid=(n_tiles + 1,),
        in_specs=([x_spec] + [_resident(c.shape) for c in consts]
                  + [pl.BlockSpec(memory_space=pl.ANY)] * len(hbm_weights)),
        out_specs=o_spec,
        out_shape=jax.ShapeDtypeStruct(x.shape, x.dtype),
        scratch_shapes=[pltpu.VMEM((CONV_HIST, d_conv), _F32),
                        pltpu.VMEM((POOL_HIST, d_conv), _F32),
                        pltpu.VMEM((seq_tile, d_in_proj), _F32)] + slots
                       + [pltpu.VMEM(m.shape, _BF16) for m in hbm_weights]
                       + [pltpu.VMEM((CAST_STAGES, CAST_ROWS, stage_cols), _F32),
                          pltpu.SemaphoreType.DMA((CAST_STAGES,))],
        compiler_params=pltpu.CompilerParams(
            dimension_semantics=("arbitrary",),
            vmem_limit_bytes=VMEM_LIMIT_BYTES),
        name="block",
    )(x, *consts, *hbm_weights)


def kernel(x, meta_tokens, norm_mix_pre, w_in, conv_w, pool_w, pool_scale, w_out,
           norm_mix_post, norm_ffn_pre, w_gate, w_up, w_down, norm_ffn_post):
    return _forward(x, meta_tokens, norm_mix_pre, w_in, conv_w, pool_w, pool_scale,
                    w_out, norm_mix_post, norm_ffn_pre, w_gate, w_up, w_down,
                    norm_ffn_post)
```

```python
import functools

import jax
import jax.numpy as jnp
from jax import lax
from jax.experimental import pallas as pl
from jax.experimental.pallas import tpu as pltpu

RMS_EPS = 1e-6
POOL_WINDOWS = (2, 4, 8, 16)
N_META = 16
SUBLANES = 8
LANES = 128
MXU_COLS = 256
CONV_HIST = SUBLANES
POOL_HIST = 16
SEQ_TILE = 512
POST_ROWS = 16
MIX_IN_LO_AT, MIX_IN_HI_AT, MIX_POOL_AT, MIX_OUT_AT, MIX_POST_AT = 2, 3, 5, 7, 9
DOWN_GROUP = 3
DOWN_LAG = 2
VMEM_LIMIT_BYTES = 56 * 1024 * 1024

_BF16 = jnp.bfloat16
_F32 = jnp.float32


def _rms_norm(x, g):
    y = x * lax.rsqrt(jnp.mean(x * x, axis=-1, keepdims=True) + RMS_EPS)
    return y * g


def _dot(a, b):
    return jnp.dot(a, b, preferred_element_type=_F32)


def _project_in_half(a, win_ref, half):
    n = win_ref.shape[1] // 2
    return _dot(a, win_ref[:, half * n:(half + 1) * n])


def _split_projection(z_lo, z_hi):
    d_conv = z_lo.shape[1] // 2
    return z_lo[:, :d_conv], z_lo[:, d_conv:] * z_hi[:, :d_conv], z_hi[:, d_conv:]


def _project_in(x, g_ref, win_ref):
    a = _rms_norm(x, g_ref[...]).astype(_BF16)
    return _split_projection(_project_in_half(a, win_ref, 0), _project_in_half(a, win_ref, 1))


def _prep_kernel(meta_ref, g_ref, win_ref, poolw_ref, pscale_ref, wout_ref,
                 cv_ref, p_ref, wout_fold_ref, win_bf16_ref):
    win_bf16_ref[...] = win_ref[...].astype(_BF16)
    _, cv, p = _project_in(meta_ref[...], g_ref, win_bf16_ref)
    cv_ref[...] = cv[N_META - CONV_HIST:]
    p_ref[...] = p[N_META - POOL_HIST:]
    d_conv = pscale_ref.shape[1]
    wout_fold_ref[0:d_conv, :] = wout_ref[0:d_conv, :].astype(_BF16)
    for g in range(poolw_ref.shape[0]):
        cols = slice(g * LANES, (g + 1) * LANES)
        rows = slice(d_conv + g * LANES, d_conv + (g + 1) * LANES)
        scaled = poolw_ref[g] * pscale_ref[:, cols]
        wout_fold_ref[rows, :] = jnp.dot(scaled, wout_ref[rows, :],
                                         preferred_element_type=_F32,
                                         precision=lax.Precision.HIGHEST).astype(_BF16)


def _causal_conv(cv_hist, cv, w):
    u = jnp.concatenate([cv_hist, cv], axis=0)
    k = w.shape[0]
    y = w[0:1] * pltpu.roll(u, k - 1, 0)
    for i in range(1, k):
        shifted = pltpu.roll(u, k - 1 - i, 0) if i < k - 1 else u
        y = y + w[i:i + 1] * shifted
    return y[CONV_HIST:]


def _trailing_mean_minus_token(p_hist, p):
    group = p.shape[1] // len(POOL_WINDOWS)
    s = jnp.concatenate([p_hist, p], axis=0)
    means = []
    win = 1
    for target in POOL_WINDOWS:
        while win < target:
            s = s + pltpu.roll(s, win, 0)
            win *= 2
        means.append(s[:, 0:group] * (1.0 / target))
        s = s[:, group:]
    return jnp.concatenate(means, axis=1)[POOL_HIST:] - p


def _mix_conv_pool(b_gate, cv, p, w, cv_hist_ref, p_hist_ref):
    tile = cv.shape[0]
    y_conv = b_gate * _causal_conv(cv_hist_ref[...], cv, w["convw"][...])
    pooled = _trailing_mean_minus_token(p_hist_ref[...], p)
    cv_hist_ref[...] = cv[tile - CONV_HIST:]
    p_hist_ref[...] = p[tile - POOL_HIST:]
    return y_conv.astype(_BF16), pooled.astype(_BF16)


def _zero_after(v):
    bits = lax.bitcast_convert_type(v, jnp.uint32)
    return lax.bitcast_convert_type((bits >> 16) >> 16, _F32)


def _mix_post(x_ref, m_ref, w, h_ref, f_ref, chained=True):
    g_post, g_pre = w["g_mix_post"][...], w["g_ffn_pre"][...]
    anchor = None
    for r in range(0, m_ref.shape[0], POST_ROWS):
        rows = slice(r, r + POST_ROWS)
        m = m_ref[rows, :]
        ms = jnp.mean(m * m, axis=-1, keepdims=True)
        if anchor is not None:
            ms = ms + anchor
        h = x_ref[0, rows, :] + m * lax.rsqrt(ms + RMS_EPS) * g_post
        rs = lax.rsqrt(jnp.mean(h * h, axis=-1, keepdims=True) + RMS_EPS)
        anchor = _zero_after(rs) if chained else None
        h_ref[rows, :] = h
        f_ref[rows, :] = (h * rs * g_pre).astype(_BF16)


def _swiglu_act(f, w, j):
    cols = slice(j * MXU_COLS, (j + 1) * MXU_COLS)
    gate = _dot(f, w["wg"][:, cols])
    up = _dot(f, w["wu"][:, cols])
    return (gate / (1.0 + jnp.exp(-gate)) * up).astype(_BF16)


def _swiglu_down(acts, w, j, acc):
    rows = slice(j * MXU_COLS, (j + len(acts)) * MXU_COLS)
    part = _dot(jnp.concatenate(acts, axis=1), w["wd"][rows, :])
    return part if acc is None else acc + part


_WEIGHT_NAMES = ("mcv", "mp", "g_mix_pre", "convw",
                 "wout", "g_mix_post", "g_ffn_pre", "g_ffn_post")
_HBM_WEIGHT_NAMES = ("win", "wg", "wu", "wd")
CAST_ROWS = 128
CAST_STAGES = 4


def _cast_to_vmem(jobs, stage_ref, sem_ref):
    chunks = [(src, dst, r) for src, dst in jobs for r in range(0, src.shape[0], CAST_ROWS)]
    depth = stage_ref.shape[0]

    def stage(c):
        return stage_ref.at[c % depth, :, pl.ds(0, chunks[c][0].shape[1])]

    def copy(c):
        src, _, r = chunks[c]
        return pltpu.make_async_copy(src.at[pl.ds(r, CAST_ROWS), :], stage(c), sem_ref.at[c % depth])

    for c in range(min(depth - 1, len(chunks))):
        copy(c).start()
    for c, (_, dst, r) in enumerate(chunks):
        if c + depth - 1 < len(chunks):
            copy(c + depth - 1).start()
        copy(c).wait()
        dst[r:r + CAST_ROWS, :] = stage(c)[...].astype(_BF16)


def _block_kernel(tiles_per_seq, x_ref, *refs):
    w = dict(zip(_WEIGHT_NAMES, refs))
    hbm = dict(zip(_HBM_WEIGHT_NAMES, refs[len(_WEIGHT_NAMES):]))
    (o_ref, cv_hist_ref, p_hist_ref, z_ref, h_slots_ref, f_slots_ref,
     w["win"], w["wg"], w["wu"], w["wd"], stage_ref, sem_ref) = refs[len(_WEIGHT_NAMES) + len(_HBM_WEIGHT_NAMES):]
    k = pl.program_id(0)
    last = pl.num_programs(0) - 1

    @pl.when(k % tiles_per_seq == 0)
    def _():
        cv_hist_ref[...] = w["mcv"][...]
        p_hist_ref[...] = w["mp"][...]

    n_chunks = w["wd"].shape[0] // MXU_COLS

    def step(h_prev_ref, f_prev_ref, h_next_ref, f_next_ref, mixer=True, swiglu=True):
        x = x_ref[0]
        f_prev = f_prev_ref[...] if swiglu else None
        mix = {}

        def mix_in_lo():
            mix["a"] = _rms_norm(x, w["g_mix_pre"][...]).astype(_BF16)
            half = z_ref.shape[1] // 2
            z_ref[:, :half] = _project_in_half(mix["a"], w["win"], 0)

        def mix_in_hi():
            half = z_ref.shape[1] // 2
            z_ref[:, half:] = _project_in_half(mix.pop("a"), w["win"], 1)

        def mix_conv_pool():
            half = z_ref.shape[1] // 2
            b_gate, cv, p = _split_projection(z_ref[:, :half], z_ref[:, half:])
            mix["y_conv"], mix["pooled"] = _mix_conv_pool(b_gate, cv, p, w, cv_hist_ref, p_hist_ref)

        def mix_out_proj():
            mix_in = jnp.concatenate([mix.pop("y_conv"), mix.pop("pooled")], axis=1)
            h_next_ref[...] = _dot(mix_in, w["wout"][...])

        def mix_post():
            _mix_post(x_ref, h_next_ref, w, h_next_ref, f_next_ref, chained=swiglu)

        before_chunk = {MIX_IN_LO_AT: mix_in_lo, MIX_IN_HI_AT: mix_in_hi,
                        MIX_POOL_AT: mix_conv_pool, MIX_OUT_AT: mix_out_proj,
                        MIX_POST_AT: mix_post}
        acc = None
        acts = []
        first = 0
        for j in range(n_chunks):
            if mixer and j in before_chunk:
                before_chunk[j]()
            if not swiglu:
                continue
            acts.append(_swiglu_act(f_prev, w, j))
            if len(acts) == DOWN_GROUP + DOWN_LAG:
                acc = _swiglu_down(acts[:DOWN_GROUP], w, first, acc)
                del acts[:DOWN_GROUP]
                first += DOWN_GROUP
        while acts:
            acc = _swiglu_down(acts[:DOWN_GROUP], w, first, acc)
            del acts[:DOWN_GROUP]
            first += DOWN_GROUP
        if swiglu:
            o_ref[0] = h_prev_ref[...] + _rms_norm(acc, w["g_ffn_post"][...])

    nxt = k % 2
    prev = 1 - nxt
    slots = (h_slots_ref.at[prev], f_slots_ref.at[prev], h_slots_ref.at[nxt], f_slots_ref.at[nxt])

    @pl.when(k == 0)
    def _():
        _cast_to_vmem([(hbm[name], w[name]) for name in _HBM_WEIGHT_NAMES], stage_ref, sem_ref)
        step(*slots, swiglu=False)

    @pl.when((k > 0) & (k < last))
    def _():
        step(*slots)

    @pl.when(k == last)
    def _():
        step(*slots, mixer=False)


def _resident(shape):
    return pl.BlockSpec(shape, lambda *_: (0,) * len(shape),
                        pipeline_mode=pl.Buffered(1))


@functools.partial(jax.jit, static_argnames=("seq_tile",))
def _forward(x, meta_tokens, norm_mix_pre, w_in, conv_w, pool_w, pool_scale,
             w_out, norm_mix_post, norm_ffn_pre, w_gate, w_up, w_down,
             norm_ffn_post, seq_tile=SEQ_TILE):
    bsz, seq, d_model = x.shape
    depth, _, d_in_proj = w_in.shape
    d_conv = d_in_proj // 4
    d_ff = w_gate.shape[2]
    assert depth == 1, "single block only: meta-token outputs are never formed"
    assert meta_tokens.shape == (N_META, d_model)
    assert pool_w.shape[1] == len(POOL_WINDOWS) and pool_w.shape[2] == LANES
    assert pool_scale.shape[1] == d_conv and w_out.shape[1] == 2 * d_conv
    assert seq % seq_tile == 0 and seq_tile % POOL_HIST == 0
    assert d_ff % MXU_COLS == 0

    g_mix_pre = norm_mix_pre.astype(_F32)
    win = w_in[0].astype(_F32)
    row = lambda v: v.reshape(1, -1).astype(_F32)

    meta_cv, meta_p, wout = pl.pallas_call(
        _prep_kernel,
        out_shape=(jax.ShapeDtypeStruct((CONV_HIST, d_conv), _F32),
                   jax.ShapeDtypeStruct((POOL_HIST, d_conv), _F32),
                   jax.ShapeDtypeStruct((2 * d_conv, d_model), _BF16)),
        scratch_shapes=[pltpu.VMEM(win.shape, _BF16)],
        name="prep",
    )(meta_tokens.astype(_F32), g_mix_pre, win, pool_w[0].astype(_F32),
      row(pool_scale[0]), w_out[0].astype(_F32))

    consts = (
        meta_cv, meta_p, g_mix_pre, conv_w[0].astype(_F32),
        wout, row(norm_mix_post[0]), row(norm_ffn_pre[0]), row(norm_ffn_post[0]),
    )
    assert len(consts) == len(_WEIGHT_NAMES)
    hbm_weights = (win, w_gate[0].astype(_F32), w_up[0].astype(_F32), w_down[0].astype(_F32))
    stage_cols = max(m.shape[1] for m in hbm_weights)
    assert all(m.shape[0] % CAST_ROWS == 0 for m in hbm_weights)

    tiles_per_seq = seq // seq_tile
    n_tiles = bsz * tiles_per_seq

    def tile_index(t):
        return (t // tiles_per_seq, t % tiles_per_seq, 0)

    x_spec = pl.BlockSpec((1, seq_tile, d_model),
                          lambda k: tile_index(jnp.minimum(k, n_tiles - 1)))
    o_spec = pl.BlockSpec((1, seq_tile, d_model),
                          lambda k: tile_index(jnp.maximum(k - 1, 0)))
    slots = [pltpu.VMEM((2, seq_tile, d_model), _F32), pltpu.VMEM((2, seq_tile, d_model), _BF16)]
    return pl.pallas_call(
        functools.partial(_block_kernel, tiles_per_seq),
        grid=(n_tiles + 1,),
        in_specs=([x_spec] + [_resident(c.shape) for c in consts]
                  + [pl.BlockSpec(memory_space=pl.ANY)] * len(hbm_weights)),
        out_specs=o_spec,
        out_shape=jax.ShapeDtypeStruct(x.shape, x.dtype),
        scratch_shapes=[pltpu.VMEM((CONV_HIST, d_conv), _F32),
                        pltpu.VMEM((POOL_HIST, d_conv), _F32),
                        pltpu.VMEM((seq_tile, d_in_proj), _F32)] + slots
                       + [pltpu.VMEM(m.shape, _BF16) for m in hbm_weights]
                       + [pltpu.VMEM((CAST_STAGES, CAST_ROWS, stage_cols), _F32),
                          pltpu.SemaphoreType.DMA((CAST_STAGES,))],
        compiler_params=pltpu.CompilerParams(
            dimension_semantics=("arbitrary",),
            vmem_limit_bytes=VMEM_LIMIT_BYTES),
        name="block",
    )(x, *consts, *hbm_weights)


def kernel(x, meta_tokens, norm_mix_pre, w_in, conv_w, pool_w, pool_scale, w_out,
           norm_mix_post, norm_ffn_pre, w_gate, w_up, w_down, norm_ffn_post):
    return _forward(x, meta_tokens, norm_mix_pre, w_in, conv_w, pool_w, pool_scale,
                    w_out, norm_mix_post, norm_ffn_pre, w_gate, w_up, w_down,
                    norm_ffn_post)
```

```python
import functools

import jax
import jax.numpy as jnp
from jax import lax
from jax.experimental import pallas as pl
from jax.experimental.pallas import tpu as pltpu

RMS_EPS = 1e-6
POOL_WINDOWS = (2, 4, 8, 16)
N_META = 16
SUBLANES = 8
LANES = 128
MXU_COLS = 256
CONV_HIST = SUBLANES
POOL_HIST = 16
SEQ_TILE = 512
POST_ROWS = 16
MIX_IN_LO_AT, MIX_IN_HI_AT, MIX_POOL_AT, MIX_OUT_AT, MIX_POST_AT = 2, 3, 5, 7, 9
DOWN_GROUP = 4
DOWN_LAG = 3
VMEM_LIMIT_BYTES = 56 * 1024 * 1024

_BF16 = jnp.bfloat16
_F32 = jnp.float32


def _rms_norm(x, g):
    y = x * lax.rsqrt(jnp.mean(x * x, axis=-1, keepdims=True) + RMS_EPS)
    return y * g


def _dot(a, b):
    return jnp.dot(a, b, preferred_element_type=_F32)


def _project_in_half(a, win_ref, half):
    n = win_ref.shape[1] // 2
    return _dot(a, win_ref[:, half * n:(half + 1) * n])


def _split_projection(z_lo, z_hi):
    d_conv = z_lo.shape[1] // 2
    return z_lo[:, :d_conv], z_lo[:, d_conv:] * z_hi[:, :d_conv], z_hi[:, d_conv:]


def _project_in(x, g_ref, win_ref):
    a = _rms_norm(x, g_ref[...]).astype(_BF16)
    return _split_projection(_project_in_half(a, win_ref, 0), _project_in_half(a, win_ref, 1))


def _prep_kernel(meta_ref, g_ref, win_ref, poolw_ref, pscale_ref, wout_ref,
                 cv_ref, p_ref, wout_fold_ref, win_bf16_ref):
    win_bf16_ref[...] = win_ref[...].astype(_BF16)
    _, cv, p = _project_in(meta_ref[...], g_ref, win_bf16_ref)
    cv_ref[...] = cv[N_META - CONV_HIST:]
    p_ref[...] = p[N_META - POOL_HIST:]
    d_conv = pscale_ref.shape[1]
    wout_fold_ref[0:d_conv, :] = wout_ref[0:d_conv, :].astype(_BF16)
    for g in range(poolw_ref.shape[0]):
        cols = slice(g * LANES, (g + 1) * LANES)
        rows = slice(d_conv + g * LANES, d_conv + (g + 1) * LANES)
        scaled = poolw_ref[g] * pscale_ref[:, cols]
        wout_fold_ref[rows, :] = jnp.dot(scaled, wout_ref[rows, :],
                                         preferred_element_type=_F32,
                                         precision=lax.Precision.HIGHEST).astype(_BF16)


def _causal_conv(cv_hist, cv, w):
    u = jnp.concatenate([cv_hist, cv], axis=0)
    k = w.shape[0]
    y = w[0:1] * pltpu.roll(u, k - 1, 0)
    for i in range(1, k):
        shifted = pltpu.roll(u, k - 1 - i, 0) if i < k - 1 else u
        y = y + w[i:i + 1] * shifted
    return y[CONV_HIST:]


def _trailing_mean_minus_token(p_hist, p):
    group = p.shape[1] // len(POOL_WINDOWS)
    s = jnp.concatenate([p_hist, p], axis=0)
    means = []
    win = 1
    for target in POOL_WINDOWS:
        while win < target:
            s = s + pltpu.roll(s, win, 0)
            win *= 2
        means.append(s[:, 0:group] * (1.0 / target))
        s = s[:, group:]
    return jnp.concatenate(means, axis=1)[POOL_HIST:] - p


def _mix_conv_pool(b_gate, cv, p, w, cv_hist_ref, p_hist_ref):
    tile = cv.shape[0]
    y_conv = b_gate * _causal_conv(cv_hist_ref[...], cv, w["convw"][...])
    pooled = _trailing_mean_minus_token(p_hist_ref[...], p)
    cv_hist_ref[...] = cv[tile - CONV_HIST:]
    p_hist_ref[...] = p[tile - POOL_HIST:]
    return y_conv.astype(_BF16), pooled.astype(_BF16)


def _zero_after(v):
    bits = lax.bitcast_convert_type(v, jnp.uint32)
    return lax.bitcast_convert_type((bits >> 16) >> 16, _F32)


def _mix_post(x_ref, m_ref, w, h_ref, f_ref, chained=True):
    g_post, g_pre = w["g_mix_post"][...], w["g_ffn_pre"][...]
    anchor = None
    for r in range(0, m_ref.shape[0], POST_ROWS):
        rows = slice(r, r + POST_ROWS)
        m = m_ref[rows, :]
        ms = jnp.mean(m * m, axis=-1, keepdims=True)
        if anchor is not None:
            ms = ms + anchor
        h = x_ref[0, rows, :] + m * lax.rsqrt(ms + RMS_EPS) * g_post
        rs = lax.rsqrt(jnp.mean(h * h, axis=-1, keepdims=True) + RMS_EPS)
        anchor = _zero_after(rs) if chained else None
        h_ref[rows, :] = h
        f_ref[rows, :] = (h * rs * g_pre).astype(_BF16)


def _swiglu_act(f, w, j):
    cols = slice(j * MXU_COLS, (j + 1) * MXU_COLS)
    gate = _dot(f, w["wg"][:, cols])
    up = _dot(f, w["wu"][:, cols])
    return (gate / (1.0 + jnp.exp(-gate)) * up).astype(_BF16)


def _swiglu_down(acts, w, j, acc):
    rows = slice(j * MXU_COLS, (j + len(acts)) * MXU_COLS)
    part = _dot(jnp.concatenate(acts, axis=1), w["wd"][rows, :])
    return part if acc is None else acc + part


_WEIGHT_NAMES = ("mcv", "mp", "g_mix_pre", "convw",
                 "wout", "g_mix_post", "g_ffn_pre", "g_ffn_post")
_HBM_WEIGHT_NAMES = ("win", "wg", "wu", "wd")
CAST_ROWS = 128
CAST_STAGES = 4


def _cast_to_vmem(jobs, stage_ref, sem_ref):
    chunks = [(src, dst, r) for src, dst in jobs for r in range(0, src.shape[0], CAST_ROWS)]
    depth = stage_ref.shape[0]

    def stage(c):
        return stage_ref.at[c % depth, :, pl.ds(0, chunks[c][0].shape[1])]

    def copy(c):
        src, _, r = chunks[c]
        return pltpu.make_async_copy(src.at[pl.ds(r, CAST_ROWS), :], stage(c), sem_ref.at[c % depth])

    for c in range(min(depth - 1, len(chunks))):
        copy(c).start()
    for c, (_, dst, r) in enumerate(chunks):
        if c + depth - 1 < len(chunks):
            copy(c + depth - 1).start()
        copy(c).wait()
        dst[r:r + CAST_ROWS, :] = stage(c)[...].astype(_BF16)


def _block_kernel(tiles_per_seq, x_ref, *refs):
    w = dict(zip(_WEIGHT_NAMES, refs))
    hbm = dict(zip(_HBM_WEIGHT_NAMES, refs[len(_WEIGHT_NAMES):]))
    (o_ref, cv_hist_ref, p_hist_ref, z_ref, h_slots_ref, f_slots_ref,
     w["win"], w["wg"], w["wu"], w["wd"], stage_ref, sem_ref) = refs[len(_WEIGHT_NAMES) + len(_HBM_WEIGHT_NAMES):]
    k = pl.program_id(0)
    last = pl.num_programs(0) - 1

    @pl.when(k % tiles_per_seq == 0)
    def _():
        cv_hist_ref[...] = w["mcv"][...]
        p_hist_ref[...] = w["mp"][...]

    n_chunks = w["wd"].shape[0] // MXU_COLS

    def step(h_prev_ref, f_prev_ref, h_next_ref, f_next_ref, mixer=True, swiglu=True):
        x = x_ref[0]
        f_prev = f_prev_ref[...] if swiglu else None
        mix = {}

        def mix_in_lo():
            mix["a"] = _rms_norm(x, w["g_mix_pre"][...]).astype(_BF16)
            half = z_ref.shape[1] // 2
            z_ref[:, :half] = _project_in_half(mix["a"], w["win"], 0)

        def mix_in_hi():
            half = z_ref.shape[1] // 2
            z_ref[:, half:] = _project_in_half(mix.pop("a"), w["win"], 1)

        def mix_conv_pool():
            half = z_ref.shape[1] // 2
            b_gate, cv, p = _split_projection(z_ref[:, :half], z_ref[:, half:])
            mix["y_conv"], mix["pooled"] = _mix_conv_pool(b_gate, cv, p, w, cv_hist_ref, p_hist_ref)

        def mix_out_proj():
            mix_in = jnp.concatenate([mix.pop("y_conv"), mix.pop("pooled")], axis=1)
            h_next_ref[...] = _dot(mix_in, w["wout"][...])

        def mix_post():
            _mix_post(x_ref, h_next_ref, w, h_next_ref, f_next_ref, chained=swiglu)

        before_chunk = {MIX_IN_LO_AT: mix_in_lo, MIX_IN_HI_AT: mix_in_hi,
                        MIX_POOL_AT: mix_conv_pool, MIX_OUT_AT: mix_out_proj,
                        MIX_POST_AT: mix_post}
        acc = None
        acts = []
        first = 0
        for j in range(n_chunks):
            if mixer and j in before_chunk:
                before_chunk[j]()
            if not swiglu:
                continue
            acts.append(_swiglu_act(f_prev, w, j))
            if len(acts) == DOWN_GROUP + DOWN_LAG:
                acc = _swiglu_down(acts[:DOWN_GROUP], w, first, acc)
                del acts[:DOWN_GROUP]
                first += DOWN_GROUP
        while acts:
            acc = _swiglu_down(acts[:DOWN_GROUP], w, first, acc)
            del acts[:DOWN_GROUP]
            first += DOWN_GROUP
        if swiglu:
            o_ref[0] = h_prev_ref[...] + _rms_norm(acc, w["g_ffn_post"][...])

    nxt = k % 2
    prev = 1 - nxt
    slots = (h_slots_ref.at[prev], f_slots_ref.at[prev], h_slots_ref.at[nxt], f_slots_ref.at[nxt])

    @pl.when(k == 0)
    def _():
        _cast_to_vmem([(hbm[name], w[name]) for name in _HBM_WEIGHT_NAMES], stage_ref, sem_ref)
        step(*slots, swiglu=False)

    @pl.when((k > 0) & (k < last))
    def _():
        step(*slots)

    @pl.when(k == last)
    def _():
        step(*slots, mixer=False)


def _resident(shape):
    return pl.BlockSpec(shape, lambda *_: (0,) * len(shape),
                        pipeline_mode=pl.Buffered(1))


@functools.partial(jax.jit, static_argnames=("seq_tile",))
def _forward(x, meta_tokens, norm_mix_pre, w_in, conv_w, pool_w, pool_scale,
             w_out, norm_mix_post, norm_ffn_pre, w_gate, w_up, w_down,
             norm_ffn_post, seq_tile=SEQ_TILE):
    bsz, seq, d_model = x.shape
    depth, _, d_in_proj = w_in.shape
    d_conv = d_in_proj // 4
    d_ff = w_gate.shape[2]
    assert depth == 1, "single block only: meta-token outputs are never formed"
    assert meta_tokens.shape == (N_META, d_model)
    assert pool_w.shape[1] == len(POOL_WINDOWS) and pool_w.shape[2] == LANES
    assert pool_scale.shape[1] == d_conv and w_out.shape[1] == 2 * d_conv
    assert seq % seq_tile == 0 and seq_tile % POOL_HIST == 0
    assert d_ff % MXU_COLS == 0

    g_mix_pre = norm_mix_pre.astype(_F32)
    win = w_in[0].astype(_F32)
    row = lambda v: v.reshape(1, -1).astype(_F32)

    meta_cv, meta_p, wout = pl.pallas_call(
        _prep_kernel,
        out_shape=(jax.ShapeDtypeStruct((CONV_HIST, d_conv), _F32),
                   jax.ShapeDtypeStruct((POOL_HIST, d_conv), _F32),
                   jax.ShapeDtypeStruct((2 * d_conv, d_model), _BF16)),
        scratch_shapes=[pltpu.VMEM(win.shape, _BF16)],
        name="prep",
    )(meta_tokens.astype(_F32), g_mix_pre, win, pool_w[0].astype(_F32),
      row(pool_scale[0]), w_out[0].astype(_F32))

    consts = (
        meta_cv, meta_p, g_mix_pre, conv_w[0].astype(_F32),
        wout, row(norm_mix_post[0]), row(norm_ffn_pre[0]), row(norm_ffn_post[0]),
    )
    assert len(consts) == len(_WEIGHT_NAMES)
    hbm_weights = (win, w_gate[0].astype(_F32), w_up[0].astype(_F32), w_down[0].astype(_F32))
    stage_cols = max(m.shape[1] for m in hbm_weights)
    assert all(m.shape[0] % CAST_ROWS == 0 for m in hbm_weights)

    tiles_per_seq = seq // seq_tile
    n_tiles = bsz * tiles_per_seq

    def tile_index(t):
        return (t // tiles_per_seq, t % tiles_per_seq, 0)

    x_spec = pl.BlockSpec((1, seq_tile, d_model),
                          lambda k: tile_index(jnp.minimum(k, n_tiles - 1)))
    o_spec = pl.BlockSpec((1, seq_tile, d_model),
                          lambda k: tile_index(jnp.maximum(k - 1, 0)))
    slots = [pltpu.VMEM((2, seq_tile, d_model), _F32), pltpu.VMEM((2, seq_tile, d_model), _BF16)]
    return pl.pallas_call(
        functools.partial(_block_kernel, tiles_per_seq),
        grid=(n_tiles + 1,),
        in_specs=([x_spec] + [_resident(c.shape) for c in consts]
                  + [pl.BlockSpec(memory_space=pl.ANY)] * len(hbm_weights)),
        out_specs=o_spec,
        out_shape=jax.ShapeDtypeStruct(x.shape, x.dtype),
        scratch_shapes=[pltpu.VMEM((CONV_HIST, d_conv), _F32),
                        pltpu.VMEM((POOL_HIST, d_conv), _F32),
                        pltpu.VMEM((seq_tile, d_in_proj), _F32)] + slots
                       + [pltpu.VMEM(m.shape, _BF16) for m in hbm_weights]
                       + [pltpu.VMEM((CAST_STAGES, CAST_ROWS, stage_cols), _F32),
                          pltpu.SemaphoreType.DMA((CAST_STAGES,))],
        compiler_params=pltpu.CompilerParams(
            dimension_semantics=("arbitrary",),
            vmem_limit_bytes=VMEM_LIMIT_BYTES),
        name="block",
    )(x, *consts, *hbm_weights)


def kernel(x, meta_tokens, norm_mix_pre, w_in, conv_w, pool_w, pool_scale, w_out,
           norm_mix_post, norm_ffn_pre, w_gate, w_up, w_down, norm_ffn_post):
    return _forward(x, meta_tokens, norm_mix_pre, w_in, conv_w, pool_w, pool_scale,
                    w_out, norm_mix_post, norm_ffn_pre, w_gate, w_up, w_down,
                    norm_ffn_post)
```

```python
import functools

import jax
import jax.numpy as jnp
from jax import lax
from jax.experimental import pallas as pl
from jax.experimental.pallas import tpu as pltpu

RMS_EPS = 1e-6
POOL_WINDOWS = (2, 4, 8, 16)
N_META = 16
SUBLANES = 8
LANES = 128
MXU_COLS = 256
CONV_HIST = SUBLANES
POOL_HIST = 16
SEQ_TILE = 512
POST_ROWS = 16
MIX_IN_LO_AT, MIX_IN_HI_AT, MIX_POOL_AT, MIX_OUT_AT, MIX_POST_AT = 2, 3, 5, 7, 9
DOWN_GROUP = 4
DOWN_LAG = 3
VMEM_LIMIT_BYTES = 56 * 1024 * 1024

_BF16 = jnp.bfloat16
_F32 = jnp.float32


def _rms_norm(x, g):
    y = x * lax.rsqrt(jnp.mean(x * x, axis=-1, keepdims=True) + RMS_EPS)
    return y * g


def _dot(a, b):
    return jnp.dot(a, b, preferred_element_type=_F32)


def _project_in_half(a, win_ref, half):
    n = win_ref.shape[1] // 2
    return _dot(a, win_ref[:, half * n:(half + 1) * n])


def _split_projection(z_lo, z_hi):
    d_conv = z_lo.shape[1] // 2
    return z_lo[:, :d_conv], z_lo[:, d_conv:] * z_hi[:, :d_conv], z_hi[:, d_conv:]


def _project_in(x, g_ref, win_ref):
    a = _rms_norm(x, g_ref[...]).astype(_BF16)
    return _split_projection(_project_in_half(a, win_ref, 0), _project_in_half(a, win_ref, 1))


def _prep_kernel(meta_ref, g_ref, win_ref, poolw_ref, pscale_ref, wout_ref,
                 cv_ref, p_ref, wout_fold_ref, win_bf16_ref):
    win_bf16_ref[...] = win_ref[...].astype(_BF16)
    _, cv, p = _project_in(meta_ref[...], g_ref, win_bf16_ref)
    cv_ref[...] = cv[N_META - CONV_HIST:]
    p_ref[...] = p[N_META - POOL_HIST:]
    d_conv = pscale_ref.shape[1]
    wout_fold_ref[0:d_conv, :] = wout_ref[0:d_conv, :].astype(_BF16)
    for g in range(poolw_ref.shape[0]):
        cols = slice(g * LANES, (g + 1) * LANES)
        rows = slice(d_conv + g * LANES, d_conv + (g + 1) * LANES)
        scaled = poolw_ref[g] * pscale_ref[:, cols]
        wout_fold_ref[rows, :] = jnp.dot(scaled, wout_ref[rows, :],
                                         preferred_element_type=_F32,
                                         precision=lax.Precision.HIGHEST).astype(_BF16)


def _causal_conv(cv_hist, cv, w):
    u = jnp.concatenate([cv_hist, cv], axis=0)
    k = w.shape[0]
    y = w[0:1] * pltpu.roll(u, k - 1, 0)
    for i in range(1, k):
        shifted = pltpu.roll(u, k - 1 - i, 0) if i < k - 1 else u
        y = y + w[i:i + 1] * shifted
    return y[CONV_HIST:]


def _trailing_mean_minus_token(p_hist, p):
    group = p.shape[1] // len(POOL_WINDOWS)
    s = jnp.concatenate([p_hist, p], axis=0)
    means = []
    win = 1
    for target in POOL_WINDOWS:
        while win < target:
            s = s + pltpu.roll(s, win, 0)
            win *= 2
        means.append(s[:, 0:group] * (1.0 / target))
        s = s[:, group:]
    return jnp.concatenate(means, axis=1)[POOL_HIST:] - p


def _mix_conv_pool(b_gate, cv, p, w, cv_hist_ref, p_hist_ref):
    tile = cv.shape[0]
    y_conv = b_gate * _causal_conv(cv_hist_ref[...], cv, w["convw"][...])
    pooled = _trailing_mean_minus_token(p_hist_ref[...], p)
    cv_hist_ref[...] = cv[tile - CONV_HIST:]
    p_hist_ref[...] = p[tile - POOL_HIST:]
    return y_conv.astype(_BF16), pooled.astype(_BF16)


def _zero_after(v):
    bits = lax.bitcast_convert_type(v, jnp.uint32)
    return lax.bitcast_convert_type((bits >> 16) >> 16, _F32)


def _mix_post(x_ref, m_ref, w, h_ref, f_ref, chained=True):
    g_post, g_pre = w["g_mix_post"][...], w["g_ffn_pre"][...]
    anchor = None
    for r in range(0, m_ref.shape[0], POST_ROWS):
        rows = slice(r, r + POST_ROWS)
        m = m_ref[rows, :]
        ms = jnp.mean(m * m, axis=-1, keepdims=True)
        if anchor is not None:
            ms = ms + anchor
        h = x_ref[0, rows, :] + m * lax.rsqrt(ms + RMS_EPS) * g_post
        rs = lax.rsqrt(jnp.mean(h * h, axis=-1, keepdims=True) + RMS_EPS)
        anchor = _zero_after(rs) if chained else None
        h_ref[rows, :] = h
        f_ref[rows, :] = (h * rs * g_pre).astype(_BF16)


def _swiglu_act(f, w, j):
    cols = slice(j * MXU_COLS, (j + 1) * MXU_COLS)
    gate = _dot(f, w["wg"][:, cols])
    up = _dot(f, w["wu"][:, cols])
    return (gate / (1.0 + jnp.exp(-gate)) * up).astype(_BF16)


def _swiglu_down(acts, w, j, acc):
    rows = slice(j * MXU_COLS, (j + len(acts)) * MXU_COLS)
    part = _dot(jnp.concatenate(acts, axis=1), w["wd"][rows, :])
    return part if acc is None else acc + part


_WEIGHT_NAMES = ("mcv", "mp", "g_mix_pre", "convw",
                 "wout", "g_mix_post", "g_ffn_pre", "g_ffn_post")
_HBM_WEIGHT_NAMES = ("win", "wg", "wu", "wd")
CAST_ROWS = 128
CAST_STAGES = 4


class _CastRing:
    def __init__(self, jobs, stage_ref, sem_ref):
        self.chunks = [(src, dst, r) for src, dst in jobs for r in range(0, src.shape[0], CAST_ROWS)]
        self.stage_ref, self.sem_ref = stage_ref, sem_ref
        self.depth = stage_ref.shape[0]
        self.done = 0
        for c in range(min(self.depth - 1, len(self.chunks))):
            self._copy(c).start()

    def _stage(self, c):
        return self.stage_ref.at[c % self.depth, :, pl.ds(0, self.chunks[c][0].shape[1])]

    def _copy(self, c):
        src, _, r = self.chunks[c]
        return pltpu.make_async_copy(src.at[pl.ds(r, CAST_ROWS), :], self._stage(c),
                                     self.sem_ref.at[c % self.depth])

    def advance(self, n):
        for c in range(self.done, min(self.done + n, len(self.chunks))):
            if c + self.depth - 1 < len(self.chunks):
                self._copy(c + self.depth - 1).start()
            self._copy(c).wait()
            _, dst, r = self.chunks[c]
            dst[r:r + CAST_ROWS, :] = self._stage(c)[...].astype(_BF16)
            self.done = c + 1

    def finish(self):
        self.advance(len(self.chunks))


def _block_kernel(tiles_per_seq, x_ref, *refs):
    w = dict(zip(_WEIGHT_NAMES, refs))
    hbm = dict(zip(_HBM_WEIGHT_NAMES, refs[len(_WEIGHT_NAMES):]))
    (o_ref, cv_hist_ref, p_hist_ref, z_ref, h_slots_ref, f_slots_ref,
     w["win"], w["wg"], w["wu"], w["wd"], stage_ref, sem_ref) = refs[len(_WEIGHT_NAMES) + len(_HBM_WEIGHT_NAMES):]
    k = pl.program_id(0)
    last = pl.num_programs(0) - 1

    @pl.when(k % tiles_per_seq == 0)
    def _():
        cv_hist_ref[...] = w["mcv"][...]
        p_hist_ref[...] = w["mp"][...]

    n_chunks = w["wd"].shape[0] // MXU_COLS

    def step(h_prev_ref, f_prev_ref, h_next_ref, f_next_ref, mixer=True, swiglu=True,
             between=lambda: None):
        x = x_ref[0]
        f_prev = f_prev_ref[...] if swiglu else None
        mix = {}

        def mix_in_lo():
            mix["a"] = _rms_norm(x, w["g_mix_pre"][...]).astype(_BF16)
            half = z_ref.shape[1] // 2
            z_ref[:, :half] = _project_in_half(mix["a"], w["win"], 0)

        def mix_in_hi():
            half = z_ref.shape[1] // 2
            z_ref[:, half:] = _project_in_half(mix.pop("a"), w["win"], 1)

        def mix_conv_pool():
            half = z_ref.shape[1] // 2
            b_gate, cv, p = _split_projection(z_ref[:, :half], z_ref[:, half:])
            mix["y_conv"], mix["pooled"] = _mix_conv_pool(b_gate, cv, p, w, cv_hist_ref, p_hist_ref)

        def mix_out_proj():
            mix_in = jnp.concatenate([mix.pop("y_conv"), mix.pop("pooled")], axis=1)
            h_next_ref[...] = _dot(mix_in, w["wout"][...])

        def mix_post():
            _mix_post(x_ref, h_next_ref, w, h_next_ref, f_next_ref, chained=swiglu)

        before_chunk = {MIX_IN_LO_AT: mix_in_lo, MIX_IN_HI_AT: mix_in_hi,
                        MIX_POOL_AT: mix_conv_pool, MIX_OUT_AT: mix_out_proj,
                        MIX_POST_AT: mix_post}
        acc = None
        acts = []
        first = 0
        for j in range(n_chunks):
            if mixer and j in before_chunk:
                before_chunk[j]()
            between()
            if not swiglu:
                continue
            acts.append(_swiglu_act(f_prev, w, j))
            if len(acts) == DOWN_GROUP + DOWN_LAG:
                acc = _swiglu_down(acts[:DOWN_GROUP], w, first, acc)
                del acts[:DOWN_GROUP]
                first += DOWN_GROUP
        while acts:
            acc = _swiglu_down(acts[:DOWN_GROUP], w, first, acc)
            del acts[:DOWN_GROUP]
            first += DOWN_GROUP
        if swiglu:
            o_ref[0] = h_prev_ref[...] + _rms_norm(acc, w["g_ffn_post"][...])

    nxt = k % 2
    prev = 1 - nxt
    slots = (h_slots_ref.at[prev], f_slots_ref.at[prev], h_slots_ref.at[nxt], f_slots_ref.at[nxt])

    @pl.when(k == 0)
    def _():
        ring = _CastRing([(hbm[name], w[name]) for name in _HBM_WEIGHT_NAMES], stage_ref, sem_ref)
        ring.advance(hbm["win"].shape[0] // CAST_ROWS)
        per_stage = -(-(len(ring.chunks) - ring.done) // n_chunks)
        step(*slots, swiglu=False, between=lambda: ring.advance(per_stage))
        ring.finish()

    @pl.when((k > 0) & (k < last))
    def _():
        step(*slots)

    @pl.when(k == last)
    def _():
        step(*slots, mixer=False)


def _resident(shape):
    return pl.BlockSpec(shape, lambda *_: (0,) * len(shape),
                        pipeline_mode=pl.Buffered(1))


@functools.partial(jax.jit, static_argnames=("seq_tile",))
def _forward(x, meta_tokens, norm_mix_pre, w_in, conv_w, pool_w, pool_scale,
             w_out, norm_mix_post, norm_ffn_pre, w_gate, w_up, w_down,
             norm_ffn_post, seq_tile=SEQ_TILE):
    bsz, seq, d_model = x.shape
    depth, _, d_in_proj = w_in.shape
    d_conv = d_in_proj // 4
    d_ff = w_gate.shape[2]
    assert depth == 1, "single block only: meta-token outputs are never formed"
    assert meta_tokens.shape == (N_META, d_model)
    assert pool_w.shape[1] == len(POOL_WINDOWS) and pool_w.shape[2] == LANES
    assert pool_scale.shape[1] == d_conv and w_out.shape[1] == 2 * d_conv
    assert seq % seq_tile == 0 and seq_tile % POOL_HIST == 0
    assert d_ff % MXU_COLS == 0

    g_mix_pre = norm_mix_pre.astype(_F32)
    win = w_in[0].astype(_F32)
    row = lambda v: v.reshape(1, -1).astype(_F32)

    meta_cv, meta_p, wout = pl.pallas_call(
        _prep_kernel,
        out_shape=(jax.ShapeDtypeStruct((CONV_HIST, d_conv), _F32),
                   jax.ShapeDtypeStruct((POOL_HIST, d_conv), _F32),
                   jax.ShapeDtypeStruct((2 * d_conv, d_model), _BF16)),
        scratch_shapes=[pltpu.VMEM(win.shape, _BF16)],
        name="prep",
    )(meta_tokens.astype(_F32), g_mix_pre, win, pool_w[0].astype(_F32),
      row(pool_scale[0]), w_out[0].astype(_F32))

    consts = (
        meta_cv, meta_p, g_mix_pre, conv_w[0].astype(_F32),
        wout, row(norm_mix_post[0]), row(norm_ffn_pre[0]), row(norm_ffn_post[0]),
    )
    assert len(consts) == len(_WEIGHT_NAMES)
    hbm_weights = (win, w_gate[0].astype(_F32), w_up[0].astype(_F32), w_down[0].astype(_F32))
    stage_cols = max(m.shape[1] for m in hbm_weights)
    assert all(m.shape[0] % CAST_ROWS == 0 for m in hbm_weights)

    tiles_per_seq = seq // seq_tile
    n_tiles = bsz * tiles_per_seq

    def tile_index(t):
        return (t // tiles_per_seq, t % tiles_per_seq, 0)

    x_spec = pl.BlockSpec((1, seq_tile, d_model),
                          lambda k: tile_index(jnp.minimum(k, n_tiles - 1)))
    o_spec = pl.BlockSpec((1, seq_tile, d_model),
                          lambda k: tile_index(jnp.maximum(k - 1, 0)))
    slots = [pltpu.VMEM((2, seq_tile, d_model), _F32), pltpu.VMEM((2, seq_tile, d_model), _BF16)]
    return pl.pallas_call(
        functools.partial(_block_kernel, tiles_per_seq),
        grid=(n_tiles + 1,),
        in_specs=([x_spec] + [_resident(c.shape) for c in consts]
                  + [pl.BlockSpec(memory_space=pl.ANY)] * len(hbm_weights)),
        out_specs=o_spec,
        out_shape=jax.ShapeDtypeStruct(x.shape, x.dtype),
        scratch_shapes=[pltpu.VMEM((CONV_HIST, d_conv), _F32),
                        pltpu.VMEM((POOL_HIST, d_conv), _F32),
                        pltpu.VMEM((seq_tile, d_in_proj), _F32)] + slots
                       + [pltpu.VMEM(m.shape, _BF16) for m in hbm_weights]
                       + [pltpu.VMEM((CAST_STAGES, CAST_ROWS, stage_cols), _F32),
                          pltpu.SemaphoreType.DMA((CAST_STAGES,))],
        compiler_params=pltpu.CompilerParams(
            dimension_semantics=("arbitrary",),
            vmem_limit_bytes=VMEM_LIMIT_BYTES),
        name="block",
    )(x, *consts, *hbm_weights)


def kernel(x, meta_tokens, norm_mix_pre, w_in, conv_w, pool_w, pool_scale, w_out,
           norm_mix_post, norm_ffn_pre, w_gate, w_up, w_down, norm_ffn_post):
    return _forward(x, meta_tokens, norm_mix_pre, w_in, conv_w, pool_w, pool_scale,
                    w_out, norm_mix_post, norm_ffn_pre, w_gate, w_up, w_down,
                    norm_ffn_post)
```

```python
import functools

import jax
import jax.numpy as jnp
from jax import lax
from jax.experimental import pallas as pl
from jax.experimental.pallas import tpu as pltpu

RMS_EPS = 1e-6
POOL_WINDOWS = (2, 4, 8, 16)
N_META = 16
SUBLANES = 8
LANES = 128
MXU_COLS = 256
CONV_HIST = SUBLANES
POOL_HIST = 16
SEQ_TILE = 512
POST_ROWS = 16
MIX_IN_LO_AT, MIX_IN_HI_AT, MIX_POOL_AT, MIX_OUT_AT, MIX_POST_AT = 2, 3, 5, 7, 9
DOWN_GROUP = 4
TAIL_HALVES = 2
DOWN_LAG = 3
VMEM_LIMIT_BYTES = 56 * 1024 * 1024

_BF16 = jnp.bfloat16
_F32 = jnp.float32


def _rms_norm(x, g):
    y = x * lax.rsqrt(jnp.mean(x * x, axis=-1, keepdims=True) + RMS_EPS)
    return y * g


def _dot(a, b):
    return jnp.dot(a, b, preferred_element_type=_F32)


def _project_in_half(a, win_ref, half):
    n = win_ref.shape[1] // 2
    return _dot(a, win_ref[:, half * n:(half + 1) * n])


def _split_projection(z_lo, z_hi):
    d_conv = z_lo.shape[1] // 2
    return z_lo[:, :d_conv], z_lo[:, d_conv:] * z_hi[:, :d_conv], z_hi[:, d_conv:]


def _project_in(x, g_ref, win_ref):
    a = _rms_norm(x, g_ref[...]).astype(_BF16)
    return _split_projection(_project_in_half(a, win_ref, 0), _project_in_half(a, win_ref, 1))


def _prep_kernel(meta_ref, g_ref, win_ref, poolw_ref, pscale_ref, wout_ref,
                 cv_ref, p_ref, wout_fold_ref, win_bf16_ref):
    win_bf16_ref[...] = win_ref[...].astype(_BF16)
    _, cv, p = _project_in(meta_ref[...], g_ref, win_bf16_ref)
    cv_ref[...] = cv[N_META - CONV_HIST:]
    p_ref[...] = p[N_META - POOL_HIST:]
    d_conv = pscale_ref.shape[1]
    wout_fold_ref[0:d_conv, :] = wout_ref[0:d_conv, :].astype(_BF16)
    for g in range(poolw_ref.shape[0]):
        cols = slice(g * LANES, (g + 1) * LANES)
        rows = slice(d_conv + g * LANES, d_conv + (g + 1) * LANES)
        scaled = poolw_ref[g] * pscale_ref[:, cols]
        wout_fold_ref[rows, :] = jnp.dot(scaled, wout_ref[rows, :],
                                         preferred_element_type=_F32,
                                         precision=lax.Precision.HIGHEST).astype(_BF16)


def _causal_conv(cv_hist, cv, w):
    u = jnp.concatenate([cv_hist, cv], axis=0)
    k = w.shape[0]
    y = w[0:1] * pltpu.roll(u, k - 1, 0)
    for i in range(1, k):
        shifted = pltpu.roll(u, k - 1 - i, 0) if i < k - 1 else u
        y = y + w[i:i + 1] * shifted
    return y[CONV_HIST:]


def _trailing_mean_minus_token(p_hist, p):
    group = p.shape[1] // len(POOL_WINDOWS)
    s = jnp.concatenate([p_hist, p], axis=0)
    means = []
    win = 1
    for target in POOL_WINDOWS:
        while win < target:
            s = s + pltpu.roll(s, win, 0)
            win *= 2
        means.append(s[:, 0:group] * (1.0 / target))
        s = s[:, group:]
    return jnp.concatenate(means, axis=1)[POOL_HIST:] - p


def _mix_conv_pool(b_gate, cv, p, w, cv_hist_ref, p_hist_ref):
    tile = cv.shape[0]
    y_conv = b_gate * _causal_conv(cv_hist_ref[...], cv, w["convw"][...])
    pooled = _trailing_mean_minus_token(p_hist_ref[...], p)
    cv_hist_ref[...] = cv[tile - CONV_HIST:]
    p_hist_ref[...] = p[tile - POOL_HIST:]
    return y_conv.astype(_BF16), pooled.astype(_BF16)


def _zero_after(v):
    bits = lax.bitcast_convert_type(v, jnp.uint32)
    return lax.bitcast_convert_type((bits >> 16) >> 16, _F32)


def _mix_post(x_ref, m_ref, w, h_ref, f_ref, chained=True):
    g_post, g_pre = w["g_mix_post"][...], w["g_ffn_pre"][...]
    anchor = None
    for r in range(0, m_ref.shape[0], POST_ROWS):
        rows = slice(r, r + POST_ROWS)
        m = m_ref[rows, :]
        ms = jnp.mean(m * m, axis=-1, keepdims=True)
        if anchor is not None:
            ms = ms + anchor
        h = x_ref[0, rows, :] + m * lax.rsqrt(ms + RMS_EPS) * g_post
        rs = lax.rsqrt(jnp.mean(h * h, axis=-1, keepdims=True) + RMS_EPS)
        anchor = _zero_after(rs) if chained else None
        h_ref[rows, :] = h
        f_ref[rows, :] = (h * rs * g_pre).astype(_BF16)


def _swiglu_act(f, w, j):
    cols = slice(j * MXU_COLS, (j + 1) * MXU_COLS)
    gate = _dot(f, w["wg"][:, cols])
    up = _dot(f, w["wu"][:, cols])
    return (gate / (1.0 + jnp.exp(-gate)) * up).astype(_BF16)


def _swiglu_down(acts, w, j, acc):
    rows = slice(j * MXU_COLS, (j + len(acts)) * MXU_COLS)
    part = _dot(jnp.concatenate(acts, axis=1), w["wd"][rows, :])
    return part if acc is None else acc + part


_WEIGHT_NAMES = ("mcv", "mp", "g_mix_pre", "convw",
                 "wout", "g_mix_post", "g_ffn_pre", "g_ffn_post")
_HBM_WEIGHT_NAMES = ("win", "wg", "wu", "wd")
CAST_ROWS = 128
CAST_STAGES = 4


def _cast_to_vmem(jobs, stage_ref, sem_ref):
    chunks = [(src, dst, r) for src, dst in jobs for r in range(0, src.shape[0], CAST_ROWS)]
    depth = stage_ref.shape[0]

    def stage(c):
        return stage_ref.at[c % depth, :, pl.ds(0, chunks[c][0].shape[1])]

    def copy(c):
        src, _, r = chunks[c]
        return pltpu.make_async_copy(src.at[pl.ds(r, CAST_ROWS), :], stage(c), sem_ref.at[c % depth])

    for c in range(min(depth - 1, len(chunks))):
        copy(c).start()
    for c, (_, dst, r) in enumerate(chunks):
        if c + depth - 1 < len(chunks):
            copy(c + depth - 1).start()
        copy(c).wait()
        dst[r:r + CAST_ROWS, :] = stage(c)[...].astype(_BF16)


def _block_kernel(tiles_per_seq, x_ref, *refs):
    w = dict(zip(_WEIGHT_NAMES, refs))
    hbm = dict(zip(_HBM_WEIGHT_NAMES, refs[len(_WEIGHT_NAMES):]))
    (o_ref, cv_hist_ref, p_hist_ref, z_ref, h_slots_ref, f_slots_ref,
     w["win"], w["wg"], w["wu"], w["wd"], stage_ref, sem_ref) = refs[len(_WEIGHT_NAMES) + len(_HBM_WEIGHT_NAMES):]
    k = pl.program_id(0)
    last = pl.num_programs(0) - 1

    @pl.when(k % tiles_per_seq == 0)
    def _():
        cv_hist_ref[...] = w["mcv"][...]
        p_hist_ref[...] = w["mp"][...]

    n_chunks = w["wd"].shape[0] // MXU_COLS

    def step(h_prev_ref, f_prev_ref, h_next_ref, f_next_ref, mixer=True, swiglu=True):
        x = x_ref[0]
        f_prev = f_prev_ref[...] if swiglu else None
        mix = {}

        def mix_in_lo():
            mix["a"] = _rms_norm(x, w["g_mix_pre"][...]).astype(_BF16)
            half = z_ref.shape[1] // 2
            z_ref[:, :half] = _project_in_half(mix["a"], w["win"], 0)

        def mix_in_hi():
            half = z_ref.shape[1] // 2
            z_ref[:, half:] = _project_in_half(mix.pop("a"), w["win"], 1)

        def mix_conv_pool():
            half = z_ref.shape[1] // 2
            b_gate, cv, p = _split_projection(z_ref[:, :half], z_ref[:, half:])
            mix["y_conv"], mix["pooled"] = _mix_conv_pool(b_gate, cv, p, w, cv_hist_ref, p_hist_ref)

        def mix_out_proj():
            mix_in = jnp.concatenate([mix.pop("y_conv"), mix.pop("pooled")], axis=1)
            h_next_ref[...] = _dot(mix_in, w["wout"][...])

        def mix_post():
            _mix_post(x_ref, h_next_ref, w, h_next_ref, f_next_ref, chained=swiglu)

        before_chunk = {MIX_IN_LO_AT: mix_in_lo, MIX_IN_HI_AT: mix_in_hi,
                        MIX_POOL_AT: mix_conv_pool, MIX_OUT_AT: mix_out_proj,
                        MIX_POST_AT: mix_post}
        acc = None
        acts = []
        first = 0
        for j in range(n_chunks):
            if mixer and j in before_chunk:
                before_chunk[j]()
            if not swiglu:
                continue
            acts.append(_swiglu_act(f_prev, w, j))
            if len(acts) == DOWN_GROUP + DOWN_LAG and j < n_chunks - 1:
                acc = _swiglu_down(acts[:DOWN_GROUP], w, first, acc)
                del acts[:DOWN_GROUP]
                first += DOWN_GROUP
        if swiglu:
            tile = x_ref.shape[1]
            halves = [slice(r, r + tile // TAIL_HALVES) for r in range(0, tile, tile // TAIL_HALVES)]
            for rows in halves:
                acc_rows = acc[rows]
                for g in range(0, len(acts), DOWN_GROUP):
                    group = [a[rows] for a in acts[g:g + DOWN_GROUP]]
                    acc_rows = _swiglu_down(group, w, first + g, acc_rows)
                o_ref[0, rows, :] = acc_rows
            for rows in halves:
                o_ref[0, rows, :] = h_prev_ref[rows, :] + _rms_norm(o_ref[0, rows, :], w["g_ffn_post"][...])

    nxt = k % 2
    prev = 1 - nxt
    slots = (h_slots_ref.at[prev], f_slots_ref.at[prev], h_slots_ref.at[nxt], f_slots_ref.at[nxt])

    @pl.when(k == 0)
    def _():
        _cast_to_vmem([(hbm[name], w[name]) for name in _HBM_WEIGHT_NAMES], stage_ref, sem_ref)
        step(*slots, swiglu=False)

    @pl.when((k > 0) & (k < last))
    def _():
        step(*slots)

    @pl.when(k == last)
    def _():
        step(*slots, mixer=False)


def _resident(shape):
    return pl.BlockSpec(shape, lambda *_: (0,) * len(shape),
                        pipeline_mode=pl.Buffered(1))


@functools.partial(jax.jit, static_argnames=("seq_tile",))
def _forward(x, meta_tokens, norm_mix_pre, w_in, conv_w, pool_w, pool_scale,
             w_out, norm_mix_post, norm_ffn_pre, w_gate, w_up, w_down,
             norm_ffn_post, seq_tile=SEQ_TILE):
    bsz, seq, d_model = x.shape
    depth, _, d_in_proj = w_in.shape
    d_conv = d_in_proj // 4
    d_ff = w_gate.shape[2]
    assert depth == 1, "single block only: meta-token outputs are never formed"
    assert meta_tokens.shape == (N_META, d_model)
    assert pool_w.shape[1] == len(POOL_WINDOWS) and pool_w.shape[2] == LANES
    assert pool_scale.shape[1] == d_conv and w_out.shape[1] == 2 * d_conv
    assert seq % seq_tile == 0 and seq_tile % POOL_HIST == 0
    assert d_ff % MXU_COLS == 0

    g_mix_pre = norm_mix_pre.astype(_F32)
    win = w_in[0].astype(_F32)
    row = lambda v: v.reshape(1, -1).astype(_F32)

    meta_cv, meta_p, wout = pl.pallas_call(
        _prep_kernel,
        out_shape=(jax.ShapeDtypeStruct((CONV_HIST, d_conv), _F32),
                   jax.ShapeDtypeStruct((POOL_HIST, d_conv), _F32),
                   jax.ShapeDtypeStruct((2 * d_conv, d_model), _BF16)),
        scratch_shapes=[pltpu.VMEM(win.shape, _BF16)],
        name="prep",
    )(meta_tokens.astype(_F32), g_mix_pre, win, pool_w[0].astype(_F32),
      row(pool_scale[0]), w_out[0].astype(_F32))

    consts = (
        meta_cv, meta_p, g_mix_pre, conv_w[0].astype(_F32),
        wout, row(norm_mix_post[0]), row(norm_ffn_pre[0]), row(norm_ffn_post[0]),
    )
    assert len(consts) == len(_WEIGHT_NAMES)
    hbm_weights = (win, w_gate[0].astype(_F32), w_up[0].astype(_F32), w_down[0].astype(_F32))
    stage_cols = max(m.shape[1] for m in hbm_weights)
    assert all(m.shape[0] % CAST_ROWS == 0 for m in hbm_weights)

    tiles_per_seq = seq // seq_tile
    n_tiles = bsz * tiles_per_seq

    def tile_index(t):
        return (t // tiles_per_seq, t % tiles_per_seq, 0)

    x_spec = pl.BlockSpec((1, seq_tile, d_model),
                          lambda k: tile_index(jnp.minimum(k, n_tiles - 1)))
    o_spec = pl.BlockSpec((1, seq_tile, d_model),
                          lambda k: tile_index(jnp.maximum(k - 1, 0)))
    slots = [pltpu.VMEM((2, seq_tile, d_model), _F32), pltpu.VMEM((2, seq_tile, d_model), _BF16)]
    return pl.pallas_call(
        functools.partial(_block_kernel, tiles_per_seq),
        grid=(n_tiles + 1,),
        in_specs=([x_spec] + [_resident(c.shape) for c in consts]
                  + [pl.BlockSpec(memory_space=pl.ANY)] * len(hbm_weights)),
        out_specs=o_spec,
        out_shape=jax.ShapeDtypeStruct(x.shape, x.dtype),
        scratch_shapes=[pltpu.VMEM((CONV_HIST, d_conv), _F32),
                        pltpu.VMEM((POOL_HIST, d_conv), _F32),
                        pltpu.VMEM((seq_tile, d_in_proj), _F32)] + slots
                       + [pltpu.VMEM(m.shape, _BF16) for m in hbm_weights]
                       + [pltpu.VMEM((CAST_STAGES, CAST_ROWS, stage_cols), _F32),
                          pltpu.SemaphoreType.DMA((CAST_STAGES,))],
        compiler_params=pltpu.CompilerParams(
            dimension_semantics=("arbitrary",),
            vmem_limit_bytes=VMEM_LIMIT_BYTES),
        name="block",
    )(x, *consts, *hbm_weights)


def kernel(x, meta_tokens, norm_mix_pre, w_in, conv_w, pool_w, pool_scale, w_out,
           norm_mix_post, norm_ffn_pre, w_gate, w_up, w_down, norm_ffn_post):
    return _forward(x, meta_tokens, norm_mix_pre, w_in, conv_w, pool_w, pool_scale,
                    w_out, norm_mix_post, norm_ffn_pre, w_gate, w_up, w_down,
                    norm_ffn_post)
```

```python
import functools

import jax
import jax.numpy as jnp
from jax import lax
from jax.experimental import pallas as pl
from jax.experimental.pallas import tpu as pltpu

RMS_EPS = 1e-6
POOL_WINDOWS = (2, 4, 8, 16)
N_META = 16
SUBLANES = 8
LANES = 128
MXU_COLS = 256
CONV_HIST = SUBLANES
POOL_HIST = 16
SEQ_TILE = 512
POST_ROWS = 16
MIX_IN_LO_AT, MIX_IN_HI_AT, MIX_POOL_AT, MIX_OUT_AT, MIX_POST_AT = 2, 3, 5, 7, 9
DOWN_GROUP = 4
DOWN_LAG = 3
VMEM_LIMIT_BYTES = 56 * 1024 * 1024

_BF16 = jnp.bfloat16
_F32 = jnp.float32


def _rms_norm(x, g):
    y = x * lax.rsqrt(jnp.mean(x * x, axis=-1, keepdims=True) + RMS_EPS)
    return y * g


def _dot(a, b):
    return jnp.dot(a, b, preferred_element_type=_F32)


def _project_in_half(a, win_ref, half):
    n = win_ref.shape[1] // 2
    return _dot(a, win_ref[:, half * n:(half + 1) * n])


def _split_projection(z_lo, z_hi):
    d_conv = z_lo.shape[1] // 2
    return z_lo[:, :d_conv], z_lo[:, d_conv:] * z_hi[:, :d_conv], z_hi[:, d_conv:]


def _project_in(x, g_ref, win_ref):
    a = _rms_norm(x, g_ref[...]).astype(_BF16)
    return _split_projection(_project_in_half(a, win_ref, 0), _project_in_half(a, win_ref, 1))


def _prep_kernel(meta_ref, g_ref, win_ref, poolw_ref, pscale_ref, wout_ref,
                 cv_ref, p_ref, wout_fold_ref, win_bf16_ref):
    win_bf16_ref[...] = win_ref[...].astype(_BF16)
    _, cv, p = _project_in(meta_ref[...], g_ref, win_bf16_ref)
    cv_ref[...] = cv[N_META - CONV_HIST:]
    p_ref[...] = p[N_META - POOL_HIST:]
    d_conv = pscale_ref.shape[1]
    wout_fold_ref[0:d_conv, :] = wout_ref[0:d_conv, :].astype(_BF16)
    for g in range(poolw_ref.shape[0]):
        cols = slice(g * LANES, (g + 1) * LANES)
        rows = slice(d_conv + g * LANES, d_conv + (g + 1) * LANES)
        scaled = poolw_ref[g] * pscale_ref[:, cols]
        wout_fold_ref[rows, :] = jnp.dot(scaled, wout_ref[rows, :],
                                         preferred_element_type=_F32,
                                         precision=lax.Precision.HIGHEST).astype(_BF16)


def _causal_conv(cv_hist, cv, w):
    u = jnp.concatenate([cv_hist, cv], axis=0)
    k = w.shape[0]
    y = w[0:1] * pltpu.roll(u, k - 1, 0)
    for i in range(1, k):
        shifted = pltpu.roll(u, k - 1 - i, 0) if i < k - 1 else u
        y = y + w[i:i + 1] * shifted
    return y[CONV_HIST:]


def _trailing_mean_minus_token(p_hist, p):
    group = p.shape[1] // len(POOL_WINDOWS)
    s = jnp.concatenate([p_hist, p], axis=0)
    means = []
    win = 1
    for target in POOL_WINDOWS:
        while win < target:
            s = s + pltpu.roll(s, win, 0)
            win *= 2
        means.append(s[:, 0:group] * (1.0 / target))
        s = s[:, group:]
    return jnp.concatenate(means, axis=1)[POOL_HIST:] - p


def _mix_conv_pool(b_gate, cv, p, w, cv_hist_ref, p_hist_ref):
    tile = cv.shape[0]
    y_conv = b_gate * _causal_conv(cv_hist_ref[...], cv, w["convw"][...])
    pooled = _trailing_mean_minus_token(p_hist_ref[...], p)
    cv_hist_ref[...] = cv[tile - CONV_HIST:]
    p_hist_ref[...] = p[tile - POOL_HIST:]
    return y_conv.astype(_BF16), pooled.astype(_BF16)


def _zero_after(v):
    bits = lax.bitcast_convert_type(v, jnp.uint32)
    return lax.bitcast_convert_type((bits >> 16) >> 16, _F32)


def _mix_post(x_ref, m_ref, w, h_ref, f_ref, chained=True):
    g_post, g_pre = w["g_mix_post"][...], w["g_ffn_pre"][...]
    anchor = None
    for r in range(0, m_ref.shape[0], POST_ROWS):
        rows = slice(r, r + POST_ROWS)
        m = m_ref[rows, :]
        ms = jnp.mean(m * m, axis=-1, keepdims=True)
        if anchor is not None:
            ms = ms + anchor
        h = x_ref[0, rows, :] + m * lax.rsqrt(ms + RMS_EPS) * g_post
        rs = lax.rsqrt(jnp.mean(h * h, axis=-1, keepdims=True) + RMS_EPS)
        anchor = _zero_after(rs) if chained else None
        h_ref[rows, :] = h
        f_ref[rows, :] = (h * rs * g_pre).astype(_BF16)


def _swiglu_act(f, w, j):
    cols = slice(j * MXU_COLS, (j + 1) * MXU_COLS)
    gate = _dot(f, w["wg"][:, cols])
    up = _dot(f, w["wu"][:, cols])
    return (gate / (1.0 + jnp.exp(-gate)) * up).astype(_BF16)


def _swiglu_down(acts, w, j, acc):
    rows = slice(j * MXU_COLS, (j + len(acts)) * MXU_COLS)
    part = _dot(jnp.concatenate(acts, axis=1), w["wd"][rows, :])
    return part if acc is None else acc + part


_WEIGHT_NAMES = ("mcv", "mp", "g_mix_pre", "convw",
                 "wout", "g_mix_post", "g_ffn_pre", "g_ffn_post")
_HBM_WEIGHT_NAMES = ("win", "wg", "wu", "wd")
CAST_ROWS = 128
CAST_STAGES = 4


def _cast_to_vmem(jobs, stage_ref, sem_ref):
    chunks = [(src, dst, r) for src, dst in jobs for r in range(0, src.shape[0], CAST_ROWS)]
    depth = stage_ref.shape[0]

    def stage(c):
        return stage_ref.at[c % depth, :, pl.ds(0, chunks[c][0].shape[1])]

    def copy(c):
        src, _, r = chunks[c]
        return pltpu.make_async_copy(src.at[pl.ds(r, CAST_ROWS), :], stage(c), sem_ref.at[c % depth])

    for c in range(min(depth - 1, len(chunks))):
        copy(c).start()
    for c, (_, dst, r) in enumerate(chunks):
        if c + depth - 1 < len(chunks):
            copy(c + depth - 1).start()
        copy(c).wait()
        dst[r:r + CAST_ROWS, :] = stage(c)[...].astype(_BF16)


def _block_kernel(tiles_per_seq, x_ref, *refs):
    w = dict(zip(_WEIGHT_NAMES, refs))
    hbm = dict(zip(_HBM_WEIGHT_NAMES, refs[len(_WEIGHT_NAMES):]))
    (o_ref, cv_hist_ref, p_hist_ref, z_ref, h_slots_ref, f_slots_ref,
     w["win"], w["wg"], w["wu"], w["wd"], stage_ref, sem_ref) = refs[len(_WEIGHT_NAMES) + len(_HBM_WEIGHT_NAMES):]
    k = pl.program_id(0)
    last = pl.num_programs(0) - 1

    @pl.when(k % tiles_per_seq == 0)
    def _():
        cv_hist_ref[...] = w["mcv"][...]
        p_hist_ref[...] = w["mp"][...]

    n_chunks = w["wd"].shape[0] // MXU_COLS

    def step(h_prev_ref, f_prev_ref, h_next_ref, f_next_ref, mixer=True, swiglu=True):
        x = x_ref[0]
        f_prev = f_prev_ref[...] if swiglu else None
        mix = {}

        def mix_in_lo():
            mix["a"] = _rms_norm(x, w["g_mix_pre"][...]).astype(_BF16)
            half = z_ref.shape[1] // 2
            z_ref[:, :half] = _project_in_half(mix["a"], w["win"], 0)

        def mix_in_hi():
            half = z_ref.shape[1] // 2
            z_ref[:, half:] = _project_in_half(mix.pop("a"), w["win"], 1)

        def mix_conv_pool():
            half = z_ref.shape[1] // 2
            b_gate, cv, p = _split_projection(z_ref[:, :half], z_ref[:, half:])
            mix["y_conv"], mix["pooled"] = _mix_conv_pool(b_gate, cv, p, w, cv_hist_ref, p_hist_ref)

        def mix_out_proj():
            mix_in = jnp.concatenate([mix.pop("y_conv"), mix.pop("pooled")], axis=1)
            h_next_ref[...] = _dot(mix_in, w["wout"][...])

        def mix_post():
            _mix_post(x_ref, h_next_ref, w, h_next_ref, f_next_ref, chained=swiglu)

        before_chunk = {MIX_IN_LO_AT: mix_in_lo, MIX_IN_HI_AT: mix_in_hi,
                        MIX_POOL_AT: mix_conv_pool, MIX_OUT_AT: mix_out_proj,
                        MIX_POST_AT: mix_post}
        acc = None
        acts = []
        first = 0
        for j in range(n_chunks):
            if mixer and j in before_chunk:
                before_chunk[j]()
            if not swiglu:
                continue
            acts.append(_swiglu_act(f_prev, w, j))
            if len(acts) == DOWN_GROUP + DOWN_LAG:
                acc = _swiglu_down(acts[:DOWN_GROUP], w, first, acc)
                del acts[:DOWN_GROUP]
                first += DOWN_GROUP
        while acts:
            acc = _swiglu_down(acts[:DOWN_GROUP], w, first, acc)
            del acts[:DOWN_GROUP]
            first += DOWN_GROUP
        if swiglu:
            o_ref[0] = h_prev_ref[...] + _rms_norm(acc, w["g_ffn_post"][...])

    nxt = k % 2
    prev = 1 - nxt
    slots = (h_slots_ref.at[prev], f_slots_ref.at[prev], h_slots_ref.at[nxt], f_slots_ref.at[nxt])

    @pl.when(k == 0)
    def _():
        _cast_to_vmem([(hbm[name], w[name]) for name in _HBM_WEIGHT_NAMES], stage_ref, sem_ref)
        step(*slots, swiglu=False)

    @pl.when((k > 0) & (k < last))
    def _():
        step(*slots)

    @pl.when(k == last)
    def _():
        step(*slots, mixer=False)


def _resident(shape):
    return pl.BlockSpec(shape, lambda *_: (0,) * len(shape),
                        pipeline_mode=pl.Buffered(1))


@functools.partial(jax.jit, static_argnames=("seq_tile",))
def _forward(x, meta_tokens, norm_mix_pre, w_in, conv_w, pool_w, pool_scale,
             w_out, norm_mix_post, norm_ffn_pre, w_gate, w_up, w_down,
             norm_ffn_post, seq_tile=SEQ_TILE):
    bsz, seq, d_model = x.shape
    depth, _, d_in_proj = w_in.shape
    d_conv = d_in_proj // 4
    d_ff = w_gate.shape[2]
    assert depth == 1, "single block only: meta-token outputs are never formed"
    assert meta_tokens.shape == (N_META, d_model)
    assert pool_w.shape[1] == len(POOL_WINDOWS) and pool_w.shape[2] == LANES
    assert pool_scale.shape[1] == d_conv and w_out.shape[1] == 2 * d_conv
    assert seq % seq_tile == 0 and seq_tile % POOL_HIST == 0
    assert d_ff % MXU_COLS == 0

    g_mix_pre = norm_mix_pre.astype(_F32)
    win = w_in[0].astype(_F32)
    row = lambda v: v.reshape(1, -1).astype(_F32)

    meta_cv, meta_p, wout = pl.pallas_call(
        _prep_kernel,
        out_shape=(jax.ShapeDtypeStruct((CONV_HIST, d_conv), _F32),
                   jax.ShapeDtypeStruct((POOL_HIST, d_conv), _F32),
                   jax.ShapeDtypeStruct((2 * d_conv, d_model), _BF16)),
        scratch_shapes=[pltpu.VMEM(win.shape, _BF16)],
        name="prep",
    )(meta_tokens.astype(_F32), g_mix_pre, win, pool_w[0].astype(_F32),
      row(pool_scale[0]), w_out[0].astype(_F32))

    consts = (
        meta_cv, meta_p, g_mix_pre, conv_w[0].astype(_F32),
        wout, row(norm_mix_post[0]), row(norm_ffn_pre[0]), row(norm_ffn_post[0]),
    )
    assert len(consts) == len(_WEIGHT_NAMES)
    hbm_weights = (win, w_gate[0].astype(_F32), w_up[0].astype(_F32), w_down[0].astype(_F32))
    stage_cols = max(m.shape[1] for m in hbm_weights)
    assert all(m.shape[0] % CAST_ROWS == 0 for m in hbm_weights)

    tiles_per_seq = seq // seq_tile
    n_tiles = bsz * tiles_per_seq

    def tile_index(t):
        return (t // tiles_per_seq, t % tiles_per_seq, 0)

    x_spec = pl.BlockSpec((1, seq_tile, d_model),
                          lambda k: tile_index(jnp.minimum(k, n_tiles - 1)))
    o_spec = pl.BlockSpec((1, seq_tile, d_model),
                          lambda k: tile_index(jnp.maximum(k - 1, 0)))
    slots = [pltpu.VMEM((2, seq_tile, d_model), _F32), pltpu.VMEM((2, seq_tile, d_model), _BF16)]
    return pl.pallas_call(
        functools.partial(_block_kernel, tiles_per_seq),
        grid=(n_tiles + 1,),
        in_specs=([x_spec] + [_resident(c.shape) for c in consts]
                  + [pl.BlockSpec(memory_space=pl.ANY)] * len(hbm_weights)),
        out_specs=o_spec,
        out_shape=jax.ShapeDtypeStruct(x.shape, x.dtype),
        scratch_shapes=[pltpu.VMEM((CONV_HIST, d_conv), _F32),
                        pltpu.VMEM((POOL_HIST, d_conv), _F32),
                        pltpu.VMEM((seq_tile, d_in_proj), _F32)] + slots
                       + [pltpu.VMEM(m.shape, _BF16) for m in hbm_weights]
                       + [pltpu.VMEM((CAST_STAGES, CAST_ROWS, stage_cols), _F32),
                          pltpu.SemaphoreType.DMA((CAST_STAGES,))],
        compiler_params=pltpu.CompilerParams(
            dimension_semantics=("arbitrary",),
            vmem_limit_bytes=VMEM_LIMIT_BYTES),
        name="block",
    )(x, *consts, *hbm_weights)


def kernel(x, meta_tokens, norm_mix_pre, w_in, conv_w, pool_w, pool_scale, w_out,
           norm_mix_post, norm_ffn_pre, w_gate, w_up, w_down, norm_ffn_post):
    return _forward(x, meta_tokens, norm_mix_pre, w_in, conv_w, pool_w, pool_scale,
                    w_out, norm_mix_post, norm_ffn_pre, w_gate, w_up, w_down,
                    norm_ffn_post)
```

```python
import functools

import jax
import jax.numpy as jnp
from jax import lax
from jax.experimental import pallas as pl
from jax.experimental.pallas import tpu as pltpu

RMS_EPS = 1e-6
POOL_WINDOWS = (2, 4, 8, 16)
N_META = 16
SUBLANES = 8
LANES = 128
MXU_COLS = 256
CONV_HIST = SUBLANES
POOL_HIST = 16
SEQ_TILE = 512
POST_ROWS = 16
MIX_IN_LO_AT, MIX_IN_HI_AT, MIX_POOL_AT, MIX_OUT_AT, MIX_POST_AT = 1, 2, 4, 6, 8
DOWN_GROUP = 4
DOWN_LAG = 3
VMEM_LIMIT_BYTES = 56 * 1024 * 1024

_BF16 = jnp.bfloat16
_F32 = jnp.float32


def _rms_norm(x, g):
    y = x * lax.rsqrt(jnp.mean(x * x, axis=-1, keepdims=True) + RMS_EPS)
    return y * g


def _dot(a, b):
    return jnp.dot(a, b, preferred_element_type=_F32)


def _project_in_half(a, win_ref, half):
    n = win_ref.shape[1] // 2
    return _dot(a, win_ref[:, half * n:(half + 1) * n])


def _split_projection(z_lo, z_hi):
    d_conv = z_lo.shape[1] // 2
    return z_lo[:, :d_conv], z_lo[:, d_conv:] * z_hi[:, :d_conv], z_hi[:, d_conv:]


def _project_in(x, g_ref, win_ref):
    a = _rms_norm(x, g_ref[...]).astype(_BF16)
    return _split_projection(_project_in_half(a, win_ref, 0), _project_in_half(a, win_ref, 1))


def _prep_kernel(meta_ref, g_ref, win_ref, poolw_ref, pscale_ref, wout_ref,
                 cv_ref, p_ref, wout_fold_ref, win_bf16_ref):
    win_bf16_ref[...] = win_ref[...].astype(_BF16)
    _, cv, p = _project_in(meta_ref[...], g_ref, win_bf16_ref)
    cv_ref[...] = cv[N_META - CONV_HIST:]
    p_ref[...] = p[N_META - POOL_HIST:]
    d_conv = pscale_ref.shape[1]
    wout_fold_ref[0:d_conv, :] = wout_ref[0:d_conv, :].astype(_BF16)
    for g in range(poolw_ref.shape[0]):
        cols = slice(g * LANES, (g + 1) * LANES)
        rows = slice(d_conv + g * LANES, d_conv + (g + 1) * LANES)
        scaled = poolw_ref[g] * pscale_ref[:, cols]
        wout_fold_ref[rows, :] = jnp.dot(scaled, wout_ref[rows, :],
                                         preferred_element_type=_F32,
                                         precision=lax.Precision.HIGHEST).astype(_BF16)


def _causal_conv(cv_hist, cv, w):
    u = jnp.concatenate([cv_hist, cv], axis=0)
    k = w.shape[0]
    y = w[0:1] * pltpu.roll(u, k - 1, 0)
    for i in range(1, k):
        shifted = pltpu.roll(u, k - 1 - i, 0) if i < k - 1 else u
        y = y + w[i:i + 1] * shifted
    return y[CONV_HIST:]


def _trailing_mean_minus_token(p_hist, p):
    group = p.shape[1] // len(POOL_WINDOWS)
    s = jnp.concatenate([p_hist, p], axis=0)
    means = []
    win = 1
    for target in POOL_WINDOWS:
        while win < target:
            s = s + pltpu.roll(s, win, 0)
            win *= 2
        means.append(s[:, 0:group] * (1.0 / target))
        s = s[:, group:]
    return jnp.concatenate(means, axis=1)[POOL_HIST:] - p


def _mix_conv_pool(b_gate, cv, p, w, cv_hist_ref, p_hist_ref):
    tile = cv.shape[0]
    y_conv = b_gate * _causal_conv(cv_hist_ref[...], cv, w["convw"][...])
    pooled = _trailing_mean_minus_token(p_hist_ref[...], p)
    cv_hist_ref[...] = cv[tile - CONV_HIST:]
    p_hist_ref[...] = p[tile - POOL_HIST:]
    return y_conv.astype(_BF16), pooled.astype(_BF16)


def _zero_after(v):
    bits = lax.bitcast_convert_type(v, jnp.uint32)
    return lax.bitcast_convert_type((bits >> 16) >> 16, _F32)


def _mix_post(x_ref, m_ref, w, h_ref, f_ref, chained=True):
    g_post, g_pre = w["g_mix_post"][...], w["g_ffn_pre"][...]
    anchor = None
    for r in range(0, m_ref.shape[0], POST_ROWS):
        rows = slice(r, r + POST_ROWS)
        m = m_ref[rows, :]
        ms = jnp.mean(m * m, axis=-1, keepdims=True)
        if anchor is not None:
            ms = ms + anchor
        h = x_ref[0, rows, :] + m * lax.rsqrt(ms + RMS_EPS) * g_post
        rs = lax.rsqrt(jnp.mean(h * h, axis=-1, keepdims=True) + RMS_EPS)
        anchor = _zero_after(rs) if chained else None
        h_ref[rows, :] = h
        f_ref[rows, :] = (h * rs * g_pre).astype(_BF16)


def _swiglu_act(f, w, j):
    cols = slice(j * MXU_COLS, (j + 1) * MXU_COLS)
    gate = _dot(f, w["wg"][:, cols])
    up = _dot(f, w["wu"][:, cols])
    return (gate / (1.0 + jnp.exp(-gate)) * up).astype(_BF16)


def _swiglu_down(acts, w, j, acc):
    rows = slice(j * MXU_COLS, (j + len(acts)) * MXU_COLS)
    part = _dot(jnp.concatenate(acts, axis=1), w["wd"][rows, :])
    return part if acc is None else acc + part


_WEIGHT_NAMES = ("mcv", "mp", "g_mix_pre", "convw",
                 "wout", "g_mix_post", "g_ffn_pre", "g_ffn_post")
_HBM_WEIGHT_NAMES = ("win", "wg", "wu", "wd")
CAST_ROWS = 128
CAST_STAGES = 4


def _cast_to_vmem(jobs, stage_ref, sem_ref):
    chunks = [(src, dst, r) for src, dst in jobs for r in range(0, src.shape[0], CAST_ROWS)]
    depth = stage_ref.shape[0]

    def stage(c):
        return stage_ref.at[c % depth, :, pl.ds(0, chunks[c][0].shape[1])]

    def copy(c):
        src, _, r = chunks[c]
        return pltpu.make_async_copy(src.at[pl.ds(r, CAST_ROWS), :], stage(c), sem_ref.at[c % depth])

    for c in range(min(depth - 1, len(chunks))):
        copy(c).start()
    for c, (_, dst, r) in enumerate(chunks):
        if c + depth - 1 < len(chunks):
            copy(c + depth - 1).start()
        copy(c).wait()
        dst[r:r + CAST_ROWS, :] = stage(c)[...].astype(_BF16)


def _block_kernel(tiles_per_seq, x_ref, *refs):
    w = dict(zip(_WEIGHT_NAMES, refs))
    hbm = dict(zip(_HBM_WEIGHT_NAMES, refs[len(_WEIGHT_NAMES):]))
    (o_ref, cv_hist_ref, p_hist_ref, z_ref, h_slots_ref, f_slots_ref,
     w["win"], w["wg"], w["wu"], w["wd"], stage_ref, sem_ref) = refs[len(_WEIGHT_NAMES) + len(_HBM_WEIGHT_NAMES):]
    k = pl.program_id(0)
    last = pl.num_programs(0) - 1

    @pl.when(k % tiles_per_seq == 0)
    def _():
        cv_hist_ref[...] = w["mcv"][...]
        p_hist_ref[...] = w["mp"][...]

    n_chunks = w["wd"].shape[0] // MXU_COLS

    def step(h_prev_ref, f_prev_ref, h_next_ref, f_next_ref, mixer=True, swiglu=True):
        x = x_ref[0]
        f_prev = f_prev_ref[...] if swiglu else None
        mix = {}

        def mix_in_lo():
            mix["a"] = _rms_norm(x, w["g_mix_pre"][...]).astype(_BF16)
            half = z_ref.shape[1] // 2
            z_ref[:, :half] = _project_in_half(mix["a"], w["win"], 0)

        def mix_in_hi():
            half = z_ref.shape[1] // 2
            z_ref[:, half:] = _project_in_half(mix.pop("a"), w["win"], 1)

        def mix_conv_pool():
            half = z_ref.shape[1] // 2
            b_gate, cv, p = _split_projection(z_ref[:, :half], z_ref[:, half:])
            mix["y_conv"], mix["pooled"] = _mix_conv_pool(b_gate, cv, p, w, cv_hist_ref, p_hist_ref)

        def mix_out_proj():
            mix_in = jnp.concatenate([mix.pop("y_conv"), mix.pop("pooled")], axis=1)
            h_next_ref[...] = _dot(mix_in, w["wout"][...])

        def mix_post():
            _mix_post(x_ref, h_next_ref, w, h_next_ref, f_next_ref, chained=swiglu)

        before_chunk = {MIX_IN_LO_AT: mix_in_lo, MIX_IN_HI_AT: mix_in_hi,
                        MIX_POOL_AT: mix_conv_pool, MIX_OUT_AT: mix_out_proj,
                        MIX_POST_AT: mix_post}
        acc = None
        acts = []
        first = 0
        for j in range(n_chunks):
            if mixer and j in before_chunk:
                before_chunk[j]()
            if not swiglu:
                continue
            acts.append(_swiglu_act(f_prev, w, j))
            if len(acts) == DOWN_GROUP + DOWN_LAG:
                acc = _swiglu_down(acts[:DOWN_GROUP], w, first, acc)
                del acts[:DOWN_GROUP]
                first += DOWN_GROUP
        while acts:
            acc = _swiglu_down(acts[:DOWN_GROUP], w, first, acc)
            del acts[:DOWN_GROUP]
            first += DOWN_GROUP
        if swiglu:
            o_ref[0] = h_prev_ref[...] + _rms_norm(acc, w["g_ffn_post"][...])

    nxt = k % 2
    prev = 1 - nxt
    slots = (h_slots_ref.at[prev], f_slots_ref.at[prev], h_slots_ref.at[nxt], f_slots_ref.at[nxt])

    @pl.when(k == 0)
    def _():
        _cast_to_vmem([(hbm[name], w[name]) for name in _HBM_WEIGHT_NAMES], stage_ref, sem_ref)
        step(*slots, swiglu=False)

    @pl.when((k > 0) & (k < last))
    def _():
        step(*slots)

    @pl.when(k == last)
    def _():
        step(*slots, mixer=False)


def _resident(shape):
    return pl.BlockSpec(shape, lambda *_: (0,) * len(shape),
                        pipeline_mode=pl.Buffered(1))


@functools.partial(jax.jit, static_argnames=("seq_tile",))
def _forward(x, meta_tokens, norm_mix_pre, w_in, conv_w, pool_w, pool_scale,
             w_out, norm_mix_post, norm_ffn_pre, w_gate, w_up, w_down,
             norm_ffn_post, seq_tile=SEQ_TILE):
    bsz, seq, d_model = x.shape
    depth, _, d_in_proj = w_in.shape
    d_conv = d_in_proj // 4
    d_ff = w_gate.shape[2]
    assert depth == 1, "single block only: meta-token outputs are never formed"
    assert meta_tokens.shape == (N_META, d_model)
    assert pool_w.shape[1] == len(POOL_WINDOWS) and pool_w.shape[2] == LANES
    assert pool_scale.shape[1] == d_conv and w_out.shape[1] == 2 * d_conv
    assert seq % seq_tile == 0 and seq_tile % POOL_HIST == 0
    assert d_ff % MXU_COLS == 0

    g_mix_pre = norm_mix_pre.astype(_F32)
    win = w_in[0].astype(_F32)
    row = lambda v: v.reshape(1, -1).astype(_F32)

    meta_cv, meta_p, wout = pl.pallas_call(
        _prep_kernel,
        out_shape=(jax.ShapeDtypeStruct((CONV_HIST, d_conv), _F32),
                   jax.ShapeDtypeStruct((POOL_HIST, d_conv), _F32),
                   jax.ShapeDtypeStruct((2 * d_conv, d_model), _BF16)),
        scratch_shapes=[pltpu.VMEM(win.shape, _BF16)],
        name="prep",
    )(meta_tokens.astype(_F32), g_mix_pre, win, pool_w[0].astype(_F32),
      row(pool_scale[0]), w_out[0].astype(_F32))

    consts = (
        meta_cv, meta_p, g_mix_pre, conv_w[0].astype(_F32),
        wout, row(norm_mix_post[0]), row(norm_ffn_pre[0]), row(norm_ffn_post[0]),
    )
    assert len(consts) == len(_WEIGHT_NAMES)
    hbm_weights = (win, w_gate[0].astype(_F32), w_up[0].astype(_F32), w_down[0].astype(_F32))
    stage_cols = max(m.shape[1] for m in hbm_weights)
    assert all(m.shape[0] % CAST_ROWS == 0 for m in hbm_weights)

    tiles_per_seq = seq // seq_tile
    n_tiles = bsz * tiles_per_seq

    def tile_index(t):
        return (t // tiles_per_seq, t % tiles_per_seq, 0)

    x_spec = pl.BlockSpec((1, seq_tile, d_model),
                          lambda k: tile_index(jnp.minimum(k, n_tiles - 1)))
    o_spec = pl.BlockSpec((1, seq_tile, d_model),
                          lambda k: tile_index(jnp.maximum(k - 1, 0)))
    slots = [pltpu.VMEM((2, seq_tile, d_model), _F32), pltpu.VMEM((2, seq_tile, d_model), _BF16)]
    return pl.pallas_call(
        functools.partial(_block_kernel, tiles_per_seq),
        grid=(n_tiles + 1,),
        in_specs=([x_spec] + [_resident(c.shape) for c in consts]
                  + [pl.BlockSpec(memory_space=pl.ANY)] * len(hbm_weights)),
        out_specs=o_spec,
        out_shape=jax.ShapeDtypeStruct(x.shape, x.dtype),
        scratch_shapes=[pltpu.VMEM((CONV_HIST, d_conv), _F32),
                        pltpu.VMEM((POOL_HIST, d_conv), _F32),
                        pltpu.VMEM((seq_tile, d_in_proj), _F32)] + slots
                       + [pltpu.VMEM(m.shape, _BF16) for m in hbm_weights]
                       + [pltpu.VMEM((CAST_STAGES, CAST_ROWS, stage_cols), _F32),
                          pltpu.SemaphoreType.DMA((CAST_STAGES,))],
        compiler_params=pltpu.CompilerParams(
            dimension_semantics=("arbitrary",),
            vmem_limit_bytes=VMEM_LIMIT_BYTES),
        name="block",
    )(x, *consts, *hbm_weights)


def kernel(x, meta_tokens, norm_mix_pre, w_in, conv_w, pool_w, pool_scale, w_out,
           norm_mix_post, norm_ffn_pre, w_gate, w_up, w_down, norm_ffn_post):
    return _forward(x, meta_tokens, norm_mix_pre, w_in, conv_w, pool_w, pool_scale,
                    w_out, norm_mix_post, norm_ffn_pre, w_gate, w_up, w_down,
                    norm_ffn_post)
```

```python
import functools

import jax
import jax.numpy as jnp
from jax import lax
from jax.experimental import pallas as pl
from jax.experimental.pallas import tpu as pltpu

RMS_EPS = 1e-6
POOL_WINDOWS = (2, 4, 8, 16)
N_META = 16
SUBLANES = 8
LANES = 128
MXU_COLS = 256
CONV_HIST = SUBLANES
POOL_HIST = 16
SEQ_TILE = 512
POST_ROWS = 16
MIX_IN_LO_AT, MIX_IN_HI_AT, MIX_POOL_AT, MIX_OUT_AT, MIX_POST_AT = 1, 2, 4, 6, 8
DOWN_GROUP = 4
DOWN_LAG = 3
VMEM_LIMIT_BYTES = 56 * 1024 * 1024

_BF16 = jnp.bfloat16
_F32 = jnp.float32


def _rms_norm(x, g):
    y = x * lax.rsqrt(jnp.mean(x * x, axis=-1, keepdims=True) + RMS_EPS)
    return y * g


def _dot(a, b):
    return jnp.dot(a, b, preferred_element_type=_F32)


def _project_in_half(a, win_ref, half):
    n = win_ref.shape[1] // 2
    return _dot(a, win_ref[:, half * n:(half + 1) * n])


def _split_projection(z_lo, z_hi):
    d_conv = z_lo.shape[1] // 2
    return z_lo[:, :d_conv], z_lo[:, d_conv:] * z_hi[:, :d_conv], z_hi[:, d_conv:]


def _project_in(x, g_ref, win_ref):
    a = _rms_norm(x, g_ref[...]).astype(_BF16)
    return _split_projection(_project_in_half(a, win_ref, 0), _project_in_half(a, win_ref, 1))


def _causal_conv(cv_hist, cv, w):
    u = jnp.concatenate([cv_hist, cv], axis=0)
    k = w.shape[0]
    y = w[0:1] * pltpu.roll(u, k - 1, 0)
    for i in range(1, k):
        shifted = pltpu.roll(u, k - 1 - i, 0) if i < k - 1 else u
        y = y + w[i:i + 1] * shifted
    return y[CONV_HIST:]


def _trailing_mean_minus_token(p_hist, p):
    group = p.shape[1] // len(POOL_WINDOWS)
    s = jnp.concatenate([p_hist, p], axis=0)
    means = []
    win = 1
    for target in POOL_WINDOWS:
        while win < target:
            s = s + pltpu.roll(s, win, 0)
            win *= 2
        means.append(s[:, 0:group] * (1.0 / target))
        s = s[:, group:]
    return jnp.concatenate(means, axis=1)[POOL_HIST:] - p


def _mix_conv_pool(b_gate, cv, p, w, cv_hist_ref, p_hist_ref):
    tile = cv.shape[0]
    y_conv = b_gate * _causal_conv(cv_hist_ref[...], cv, w["convw"][...])
    pooled = _trailing_mean_minus_token(p_hist_ref[...], p)
    cv_hist_ref[...] = cv[tile - CONV_HIST:]
    p_hist_ref[...] = p[tile - POOL_HIST:]
    return y_conv.astype(_BF16), pooled.astype(_BF16)


def _zero_after(v):
    bits = lax.bitcast_convert_type(v, jnp.uint32)
    return lax.bitcast_convert_type((bits >> 16) >> 16, _F32)


def _mix_post(x_ref, m_ref, w, h_ref, f_ref, chained=True):
    g_post, g_pre = w["g_mix_post"][...], w["g_ffn_pre"][...]
    anchor = None
    for r in range(0, m_ref.shape[0], POST_ROWS):
        rows = slice(r, r + POST_ROWS)
        m = m_ref[rows, :]
        ms = jnp.mean(m * m, axis=-1, keepdims=True)
        if anchor is not None:
            ms = ms + anchor
        h = x_ref[0, rows, :] + m * lax.rsqrt(ms + RMS_EPS) * g_post
        rs = lax.rsqrt(jnp.mean(h * h, axis=-1, keepdims=True) + RMS_EPS)
        anchor = _zero_after(rs) if chained else None
        h_ref[rows, :] = h
        f_ref[rows, :] = (h * rs * g_pre).astype(_BF16)


def _swiglu_act(f, w, j):
    cols = slice(j * MXU_COLS, (j + 1) * MXU_COLS)
    gate = _dot(f, w["wg"][:, cols])
    up = _dot(f, w["wu"][:, cols])
    return (gate / (1.0 + jnp.exp(-gate)) * up).astype(_BF16)


def _swiglu_down(acts, w, j, acc):
    rows = slice(j * MXU_COLS, (j + len(acts)) * MXU_COLS)
    part = _dot(jnp.concatenate(acts, axis=1), w["wd"][rows, :])
    return part if acc is None else acc + part


_WEIGHT_NAMES = ("meta", "g_mix_pre", "convw", "poolw", "pscale",
                 "g_mix_post", "g_ffn_pre", "g_ffn_post")
_HBM_WEIGHT_NAMES = ("win", "wout", "wg", "wu", "wd")
CAST_ROWS = 128
CAST_STAGES = 4


def _cast_to_vmem(jobs, stage_ref, sem_ref):
    chunks = [(src, dst, r, fn) for src, dst, fn in jobs for r in range(0, src.shape[0], CAST_ROWS)]
    depth = stage_ref.shape[0]

    def stage(c):
        return stage_ref.at[c % depth, :, pl.ds(0, chunks[c][0].shape[1])]

    def copy(c):
        src, _, r, _ = chunks[c]
        return pltpu.make_async_copy(src.at[pl.ds(r, CAST_ROWS), :], stage(c), sem_ref.at[c % depth])

    for c in range(min(depth - 1, len(chunks))):
        copy(c).start()
    for c, (_, dst, r, fn) in enumerate(chunks):
        if c + depth - 1 < len(chunks):
            copy(c + depth - 1).start()
        copy(c).wait()
        chunk = stage(c)[...]
        dst[r:r + CAST_ROWS, :] = (chunk if fn is None else fn(chunk, r)).astype(_BF16)


def _fold_pool_into_wout(poolw_ref, pscale_ref):
    d_conv = pscale_ref.shape[1]
    assert CAST_ROWS == LANES

    def transform(chunk, first_row):
        if first_row < d_conv:
            return chunk
        g = (first_row - d_conv) // LANES
        scaled = poolw_ref[g] * pscale_ref[:, g * LANES:(g + 1) * LANES]
        return jnp.dot(scaled, chunk, preferred_element_type=_F32, precision=lax.Precision.HIGHEST)

    return transform


def _block_kernel(tiles_per_seq, x_ref, *refs):
    w = dict(zip(_WEIGHT_NAMES, refs))
    hbm = dict(zip(_HBM_WEIGHT_NAMES, refs[len(_WEIGHT_NAMES):]))
    (o_ref, cv_hist_ref, p_hist_ref, z_ref, h_slots_ref, f_slots_ref,
     w["win"], w["wout"], w["wg"], w["wu"], w["wd"], w["mcv"], w["mp"],
     stage_ref, sem_ref) = refs[len(_WEIGHT_NAMES) + len(_HBM_WEIGHT_NAMES):]
    k = pl.program_id(0)
    last = pl.num_programs(0) - 1

    @pl.when(k == 0)
    def _():
        fold = _fold_pool_into_wout(w["poolw"], w["pscale"])
        _cast_to_vmem([(hbm[name], w[name], fold if name == "wout" else None)
                       for name in _HBM_WEIGHT_NAMES], stage_ref, sem_ref)
        _, cv, p = _project_in(w["meta"][...], w["g_mix_pre"], w["win"])
        w["mcv"][...] = cv[N_META - CONV_HIST:]
        w["mp"][...] = p[N_META - POOL_HIST:]

    @pl.when(k % tiles_per_seq == 0)
    def _():
        cv_hist_ref[...] = w["mcv"][...]
        p_hist_ref[...] = w["mp"][...]

    n_chunks = w["wd"].shape[0] // MXU_COLS

    def step(h_prev_ref, f_prev_ref, h_next_ref, f_next_ref, mixer=True, swiglu=True):
        x = x_ref[0]
        f_prev = f_prev_ref[...] if swiglu else None
        mix = {}

        def mix_in_lo():
            mix["a"] = _rms_norm(x, w["g_mix_pre"][...]).astype(_BF16)
            half = z_ref.shape[1] // 2
            z_ref[:, :half] = _project_in_half(mix["a"], w["win"], 0)

        def mix_in_hi():
            half = z_ref.shape[1] // 2
            z_ref[:, half:] = _project_in_half(mix.pop("a"), w["win"], 1)

        def mix_conv_pool():
            half = z_ref.shape[1] // 2
            b_gate, cv, p = _split_projection(z_ref[:, :half], z_ref[:, half:])
            mix["y_conv"], mix["pooled"] = _mix_conv_pool(b_gate, cv, p, w, cv_hist_ref, p_hist_ref)

        def mix_out_proj():
            mix_in = jnp.concatenate([mix.pop("y_conv"), mix.pop("pooled")], axis=1)
            h_next_ref[...] = _dot(mix_in, w["wout"][...])

        def mix_post():
            _mix_post(x_ref, h_next_ref, w, h_next_ref, f_next_ref, chained=swiglu)

        before_chunk = {MIX_IN_LO_AT: mix_in_lo, MIX_IN_HI_AT: mix_in_hi,
                        MIX_POOL_AT: mix_conv_pool, MIX_OUT_AT: mix_out_proj,
                        MIX_POST_AT: mix_post}
        acc = None
        acts = []
        first = 0
        for j in range(n_chunks):
            if mixer and j in before_chunk:
                before_chunk[j]()
            if not swiglu:
                continue
            acts.append(_swiglu_act(f_prev, w, j))
            if len(acts) == DOWN_GROUP + DOWN_LAG:
                acc = _swiglu_down(acts[:DOWN_GROUP], w, first, acc)
                del acts[:DOWN_GROUP]
                first += DOWN_GROUP
        while acts:
            acc = _swiglu_down(acts[:DOWN_GROUP], w, first, acc)
            del acts[:DOWN_GROUP]
            first += DOWN_GROUP
        if swiglu:
            o_ref[0] = h_prev_ref[...] + _rms_norm(acc, w["g_ffn_post"][...])

    nxt = k % 2
    prev = 1 - nxt
    slots = (h_slots_ref.at[prev], f_slots_ref.at[prev], h_slots_ref.at[nxt], f_slots_ref.at[nxt])

    @pl.when(k == 0)
    def _():
        step(*slots, swiglu=False)

    @pl.when((k > 0) & (k < last))
    def _():
        step(*slots)

    @pl.when(k == last)
    def _():
        step(*slots, mixer=False)


def _resident(shape):
    return pl.BlockSpec(shape, lambda *_: (0,) * len(shape),
                        pipeline_mode=pl.Buffered(1))


@functools.partial(jax.jit, static_argnames=("seq_tile",))
def _forward(x, meta_tokens, norm_mix_pre, w_in, conv_w, pool_w, pool_scale,
             w_out, norm_mix_post, norm_ffn_pre, w_gate, w_up, w_down,
             norm_ffn_post, seq_tile=SEQ_TILE):
    bsz, seq, d_model = x.shape
    depth, _, d_in_proj = w_in.shape
    d_conv = d_in_proj // 4
    d_ff = w_gate.shape[2]
    assert depth == 1, "single block only: meta-token outputs are never formed"
    assert meta_tokens.shape == (N_META, d_model)
    assert pool_w.shape[1] == len(POOL_WINDOWS) and pool_w.shape[2] == LANES
    assert pool_scale.shape[1] == d_conv and w_out.shape[1] == 2 * d_conv
    assert seq % seq_tile == 0 and seq_tile % POOL_HIST == 0
    assert d_ff % MXU_COLS == 0

    row = lambda v: v.reshape(1, -1).astype(_F32)
    consts = (
        meta_tokens.astype(_F32), norm_mix_pre.astype(_F32), conv_w[0].astype(_F32),
        pool_w[0].astype(_F32), row(pool_scale[0]),
        row(norm_mix_post[0]), row(norm_ffn_pre[0]), row(norm_ffn_post[0]),
    )
    assert len(consts) == len(_WEIGHT_NAMES)
    hbm_weights = tuple(m[0].astype(_F32) for m in (w_in, w_out, w_gate, w_up, w_down))
    stage_cols = max(m.shape[1] for m in hbm_weights)
    assert all(m.shape[0] % CAST_ROWS == 0 for m in hbm_weights)

    tiles_per_seq = seq // seq_tile
    n_tiles = bsz * tiles_per_seq

    def tile_index(t):
        return (t // tiles_per_seq, t % tiles_per_seq, 0)

    x_spec = pl.BlockSpec((1, seq_tile, d_model),
                          lambda k: tile_index(jnp.minimum(k, n_tiles - 1)))
    o_spec = pl.BlockSpec((1, seq_tile, d_model),
                          lambda k: tile_index(jnp.maximum(k - 1, 0)))
    slots = [pltpu.VMEM((2, seq_tile, d_model), _F32), pltpu.VMEM((2, seq_tile, d_model), _BF16)]
    return pl.pallas_call(
        functools.partial(_block_kernel, tiles_per_seq),
        grid=(n_tiles + 1,),
        in_specs=([x_spec] + [_resident(c.shape) for c in consts]
                  + [pl.BlockSpec(memory_space=pl.ANY)] * len(hbm_weights)),
        out_specs=o_spec,
        out_shape=jax.ShapeDtypeStruct(x.shape, x.dtype),
        scratch_shapes=[pltpu.VMEM((CONV_HIST, d_conv), _F32),
                        pltpu.VMEM((POOL_HIST, d_conv), _F32),
                        pltpu.VMEM((seq_tile, d_in_proj), _F32)] + slots
                       + [pltpu.VMEM(m.shape, _BF16) for m in hbm_weights]
                       + [pltpu.VMEM((CONV_HIST, d_conv), _F32),
                          pltpu.VMEM((POOL_HIST, d_conv), _F32)]
                       + [pltpu.VMEM((CAST_STAGES, CAST_ROWS, stage_cols), _F32),
                          pltpu.SemaphoreType.DMA((CAST_STAGES,))],
        compiler_params=pltpu.CompilerParams(
            dimension_semantics=("arbitrary",),
            vmem_limit_bytes=VMEM_LIMIT_BYTES),
        name="block",
    )(x, *consts, *hbm_weights)


def kernel(x, meta_tokens, norm_mix_pre, w_in, conv_w, pool_w, pool_scale, w_out,
           norm_mix_post, norm_ffn_pre, w_gate, w_up, w_down, norm_ffn_post):
    return _forward(x, meta_tokens, norm_mix_pre, w_in, conv_w, pool_w, pool_scale,
                    w_out, norm_mix_post, norm_ffn_pre, w_gate, w_up, w_down,
                    norm_ffn_post)
```
